```python
import jax, jax.numpy as jnp
from jax import lax
import numpy as np

D_MODEL = 1024
BATCH = 16
SEQ = 4096
DEPTH = 2

CTX_LEN = 256
GRID_W = 64
HEAD_DIM = 64
ATTN_W = D_MODEL // 2
N_HEADS = ATTN_W // HEAD_DIM
Q_PER_KV = 4
KV_HEADS = N_HEADS // Q_PER_KV
KV_W = KV_HEADS * HEAD_DIM
POOL_W = D_MODEL // 2
POOL_WINDOWS = (2, 4, 8, 16)
POOL_GROUPS = len(POOL_WINDOWS)
POOL_GW = POOL_W // POOL_GROUPS
N_BRANCH = 3
BRANCH_W = 512
WINDOW = 128
BLOCK = 128
D_FF = 128 * ((8 * D_MODEL // 3 + 127) // 128)
ROPE_THETA = 10000.0
EPS = 1e-6
NEG = -1e30
SM_SCALE = HEAD_DIM ** -0.5
IN_SPLITS = (POOL_W, ATTN_W, KV_W, KV_W, ATTN_W, KV_W, KV_W)
IN_W = sum(IN_SPLITS)

kernel_name = "hybrid_pool_window_global_gqa_dit"


def rms_norm(x, g):
    xf = x.astype(jnp.float32)
    y = xf * lax.rsqrt(jnp.mean(xf * xf, axis=-1, keepdims=True) + EPS)
    return (y * g.astype(jnp.float32)).astype(x.dtype)


def modulate(x, g, shift, scale):
    return rms_norm(x, g) * (1 + scale) + shift


def heads(t, n):
    return t.reshape(t.shape[:-1] + (n, HEAD_DIM))


def split_in(z):
    offs = np.cumsum((0,) + IN_SPLITS)
    return [z[..., int(offs[i]):int(offs[i + 1])] for i in range(len(IN_SPLITS))]


def axial_rope_tables(n_tok):
    rows = n_tok // GRID_W
    row = jnp.repeat(jnp.arange(rows), GRID_W).astype(jnp.float32)
    col = jnp.tile(jnp.arange(GRID_W), rows).astype(jnp.float32)
    half = HEAD_DIM // 2
    inv = ROPE_THETA ** (-jnp.arange(0, half, 2, dtype=jnp.float32) / half)
    ang = jnp.stack([row[:, None] * inv, col[:, None] * inv], axis=1)
    return jnp.cos(ang)[:, None], jnp.sin(ang)[:, None]


def apply_rope(x, cos, sin):
    shp = x.shape
    xs = x.astype(jnp.float32).reshape(shp[:-1] + (2, 2, HEAD_DIM // 4))
    x1, x2 = xs[..., 0, :], xs[..., 1, :]
    out = jnp.stack([x1 * cos - x2 * sin, x2 * cos + x1 * sin], axis=-2)
    return out.reshape(shp).astype(x.dtype)


def multiscale_pool(u, w_grp, scale):
    Bn, L, _ = u.shape
    uf = u.astype(jnp.float32)
    cs = jnp.concatenate([jnp.zeros_like(uf[:, :1]), jnp.cumsum(uf, axis=1)], axis=1)
    t = jnp.arange(L)
    outs = []
    for gi, w in enumerate(POOL_WINDOWS):
        sl = slice(gi * POOL_GW, (gi + 1) * POOL_GW)
        lo = jnp.clip(t - w // 2, 0, L)
        hi = jnp.clip(t + w // 2, 0, L)
        csg = cs[..., sl]
        s = jnp.take(csg, hi, axis=1) - jnp.take(csg, lo, axis=1)
        cnt = (hi - lo).astype(jnp.float32)[None, :, None]
        outs.append(s / cnt - uf[..., sl])
    p = jnp.stack(outs, axis=2).astype(u.dtype)
    y = jnp.einsum('blgc,gcd->blgd', p, w_grp).reshape(Bn, L, POOL_W)
    return y * scale


def window_attention(q, k, v, kc, vc, sink):
    Bn, L = q.shape[0], q.shape[1]
    nb = L // BLOCK
    pad = ((0, 0), (WINDOW, WINDOW), (0, 0), (0, 0))
    kp = jnp.pad(k, pad).reshape(Bn, nb + 2, BLOCK, KV_HEADS, HEAD_DIM)
    vp = jnp.pad(v, pad).reshape(Bn, nb + 2, BLOCK, KV_HEADS, HEAD_DIM)
    kwin = jnp.concatenate([kp[:, :-2], kp[:, 1:-1], kp[:, 2:]], axis=2).swapaxes(0, 1)
    vwin = jnp.concatenate([vp[:, :-2], vp[:, 1:-1], vp[:, 2:]], axis=2).swapaxes(0, 1)
    qb = q.reshape(Bn, nb, BLOCK, KV_HEADS, Q_PER_KV, HEAD_DIM).swapaxes(0, 1)
    rel = (jnp.arange(3 * BLOCK)[None, :] - BLOCK) - jnp.arange(BLOCK)[:, None]
    band = jnp.abs(rel) <= WINDOW
    sink_l = sink.reshape(1, KV_HEADS, Q_PER_KV, 1, 1).astype(jnp.float32)
    n_win = 3 * BLOCK
    n_ctx = kc.shape[1]

    def step(args):
        bi, qi, ki, vi = args
        kpos = bi * BLOCK - BLOCK + jnp.arange(n_win)
        valid = band & ((kpos >= 0) & (kpos < L))[None, :]
        s_lat = jnp.einsum('bqkgd,bskd->bkgqs', qi, ki).astype(jnp.float32) * SM_SCALE
        s_lat = jnp.where(valid, s_lat, NEG)
        s_ctx = jnp.einsum('bqkgd,bskd->bkgqs', qi, kc).astype(jnp.float32) * SM_SCALE
        s_snk = jnp.broadcast_to(sink_l, s_lat.shape[:-1] + (1,))
        p = jax.nn.softmax(jnp.concatenate([s_lat, s_ctx, s_snk], axis=-1), axis=-1).astype(v.dtype)
        return (jnp.einsum('bkgqs,bskd->bqkgd', p[..., :n_win], vi)
                + jnp.einsum('bkgqs,bskd->bqkgd', p[..., n_win:n_win + n_ctx], vc))

    o = lax.map(step, (jnp.arange(nb), qb, kwin, vwin))
    return o.swapaxes(0, 1).reshape(Bn, L, ATTN_W)


def global_attention(q, k, v, kc, vc):
    Bn, L = q.shape[0], q.shape[1]
    nb = L // BLOCK
    kall = jnp.concatenate([k, kc], axis=1)
    vall = jnp.concatenate([v, vc], axis=1)
    qb = q.reshape(Bn, nb, BLOCK, KV_HEADS, Q_PER_KV, HEAD_DIM).swapaxes(0, 1)

    def step(qi):
        s = jnp.einsum('bqkgd,bskd->bkgqs', qi, kall).astype(jnp.float32) * SM_SCALE
        p = jax.nn.softmax(s, axis=-1).astype(vall.dtype)
        return jnp.einsum('bkgqs,bskd->bqkgd', p, vall)

    o = lax.map(step, qb)
    return o.swapaxes(0, 1).reshape(Bn, L, ATTN_W)


def context_attention(qc, kc, vc, sink):
    s = jnp.einsum('bqkgd,bskd->bkgqs', qc, kc).astype(jnp.float32) * SM_SCALE
    if sink is not None:
        sk = jnp.broadcast_to(sink.reshape(1, KV_HEADS, Q_PER_KV, 1, 1).astype(jnp.float32), s.shape[:-1] + (1,))
        s = jnp.concatenate([s, sk], axis=-1)
    p = jax.nn.softmax(s, axis=-1)[..., :kc.shape[1]].astype(vc.dtype)
    o = jnp.einsum('bkgqs,bskd->bqkgd', p, vc)
    return o.reshape(qc.shape[0], qc.shape[1], ATTN_W)


def merge_branches(h, ys, w_branch, w_gate, b_gate, w_out):
    merged = jax.nn.sigmoid(h @ w_gate[0] + b_gate[0]) * (ys[0] @ w_branch[0])
    for i in range(1, N_BRANCH):
        merged = merged + jax.nn.sigmoid(h @ w_gate[i] + b_gate[i]) * (ys[i] @ w_branch[i])
    return merged @ w_out


def dwconv3(u, w, b):
    up = jnp.pad(u, ((0, 0), (1, 1), (0, 0)))
    return up[:, :-2] * w[0] + up[:, 1:-1] * w[1] + up[:, 2:] * w[2] + b


def conv_glu(h, w_g, w_v, cw, cb, w_d):
    a = dwconv3(h @ w_g, cw, cb)
    return (jax.nn.silu(a) * (h @ w_v)) @ w_d


def setup_inputs(seed: int = 0) -> dict:
    key = jax.random.key(seed)
    ks = jax.random.split(key, 24)
    f32 = jnp.float32
    nrm = lambda k, shp, s: jax.random.normal(k, shp, f32) * s
    D = D_MODEL
    return {
        "x": nrm(ks[0], (BATCH, SEQ, D), 1.0),
        "c": nrm(ks[1], (BATCH, D), 1.0),
        "ctx": nrm(ks[2], (BATCH, CTX_LEN, D), 1.0),
        "c_ctx": nrm(ks[3], (D,), 1.0),
        "w_mod": nrm(ks[4], (DEPTH, D, 6 * D), 0.5 * D ** -0.5),
        "b_mod": nrm(ks[5], (DEPTH, 6 * D), 0.02),
        "norm1_g": 1.0 + nrm(ks[6], (DEPTH, D), 0.02),
        "norm2_g": 1.0 + nrm(ks[7], (DEPTH, D), 0.02),
        "w_in": nrm(ks[8], (DEPTH, D, IN_W), D ** -0.5),
        "w_pool_grp": nrm(ks[9], (DEPTH, POOL_GROUPS, POOL_GW, POOL_GW), POOL_GW ** -0.5),
        "pool_scale": 1.0 + nrm(ks[10], (DEPTH, POOL_W), 0.02),
        "win_sink": nrm(ks[11], (DEPTH, N_HEADS), 0.5),
        "q_norm_g": 1.0 + nrm(ks[12], (DEPTH, HEAD_DIM), 0.02),
        "k_norm_g": 1.0 + nrm(ks[13], (DEPTH, HEAD_DIM), 0.02),
        "w_branch": nrm(ks[14], (DEPTH, N_BRANCH, BRANCH_W, D), BRANCH_W ** -0.5),
        "w_gate": nrm(ks[15], (DEPTH, N_BRANCH, D, D), D ** -0.5),
        "b_gate": nrm(ks[16], (DEPTH, N_BRANCH, D), 0.02),
        "w_out": nrm(ks[17], (DEPTH, D, D), D ** -0.5),
        "w_ff_gate": nrm(ks[18], (DEPTH, D, D_FF), D ** -0.5),
        "w_ff_val": nrm(ks[19], (DEPTH, D, D_FF), D ** -0.5),
        "conv_w": nrm(ks[20], (DEPTH, 3, D_FF), 3 ** -0.5),
        "conv_b": nrm(ks[21], (DEPTH, D_FF), 0.02),
        "w_ff_down": nrm(ks[22], (DEPTH, D_FF, D), D_FF ** -0.5),
        "final_g": 1.0 + nrm(ks[23], (D,), 0.02),
    }


def reference(x, c, ctx, c_ctx, w_mod, b_mod, norm1_g, norm2_g, w_in, w_pool_grp, pool_scale,
              win_sink, q_norm_g, k_norm_g, w_branch, w_gate, b_gate, w_out,
              w_ff_gate, w_ff_val, conv_w, conv_b, w_ff_down, final_g):
    Bn, L, _ = x.shape
    C = ctx.shape[1]
    cos, sin = axial_rope_tables(L)
    for l in range(DEPTH):
        last = l == DEPTH - 1
        mod = (jax.nn.silu(c) @ w_mod[l] + b_mod[l]).reshape(Bn, 6, 1, D_MODEL)
        mod_c = (jax.nn.silu(c_ctx) @ w_mod[l] + b_mod[l]).reshape(6, 1, 1, D_MODEL)

        h = modulate(x, norm1_g[l], mod[:, 0], mod[:, 1])
        hc = modulate(ctx, norm1_g[l], mod_c[0], mod_c[1])
        p_l, wq_l, wk_l, wv_l, gq_l, gk_l, gv_l = split_in(h @ w_in[l])
        p_c, wq_c, wk_c, wv_c, gq_c, gk_c, gv_c = split_in(hc @ w_in[l])

        wkc = heads(wk_c, KV_HEADS)
        wvc = heads(wv_c, KV_HEADS)
        gkc = rms_norm(heads(gk_c, KV_HEADS), k_norm_g[l])
        gvc = heads(gv_c, KV_HEADS)

        wq = apply_rope(heads(wq_l, N_HEADS), cos, sin).reshape(Bn, L, KV_HEADS, Q_PER_KV, HEAD_DIM)
        wk = apply_rope(heads(wk_l, KV_HEADS), cos, sin)
        gq = apply_rope(rms_norm(heads(gq_l, N_HEADS), q_norm_g[l]), cos, sin).reshape(Bn, L, KV_HEADS, Q_PER_KV, HEAD_DIM)
        gk = apply_rope(rms_norm(heads(gk_l, KV_HEADS), k_norm_g[l]), cos, sin)

        y_pool = multiscale_pool(p_l, w_pool_grp[l], pool_scale[l])
        y_win = window_attention(wq, wk, heads(wv_l, KV_HEADS), wkc, wvc, win_sink[l])
        y_glob = global_attention(gq, gk, heads(gv_l, KV_HEADS), gkc, gvc)
        x = x + mod[:, 2] * merge_branches(h, (y_pool, y_win, y_glob), w_branch[l], w_gate[l], b_gate[l], w_out[l])

        if not last:
            qcw = heads(wq_c, N_HEADS).reshape(Bn, C, KV_HEADS, Q_PER_KV, HEAD_DIM)
            qcg = rms_norm(heads(gq_c, N_HEADS), q_norm_g[l]).reshape(Bn, C, KV_HEADS, Q_PER_KV, HEAD_DIM)
            yc_pool = multiscale_pool(p_c, w_pool_grp[l], pool_scale[l])
            yc_win = context_attention(qcw, wkc, wvc, win_sink[l])
            yc_glob = context_attention(qcg, gkc, gvc, None)
            ctx = ctx + mod_c[2] * merge_branches(hc, (yc_pool, yc_win, yc_glob), w_branch[l], w_gate[l], b_gate[l], w_out[l])
            h2c = modulate(ctx, norm2_g[l], mod_c[3], mod_c[4])
            ctx = ctx + mod_c[5] * conv_glu(h2c, w_ff_gate[l], w_ff_val[l], conv_w[l], conv_b[l], w_ff_down[l])

        h2 = modulate(x, norm2_g[l], mod[:, 3], mod[:, 4])
        x = x + mod[:, 5] * conv_glu(h2, w_ff_gate[l], w_ff_val[l], conv_w[l], conv_b[l], w_ff_down[l])
    return rms_norm(x, final_g)
```

```python
import functools

import jax
import jax.numpy as jnp
import numpy as np
from jax import lax
from jax.experimental import pallas as pl
from jax.experimental.pallas import tpu as pltpu

F32 = jnp.float32
BF16 = jnp.bfloat16

GRID_W = 64
HEAD_DIM = 64
Q_PER_KV = 4
POOL_WINDOWS = (2, 4, 8, 16)
POOL_HALO = 16
WINDOW = 128
BLOCK = 128
ROPE_THETA = 10000.0
EPS = 1e-6
NEG = -1e30
SM_SCALE = HEAD_DIM ** -0.5
LANES = 128
BF16_SUBLANES = 16
VMEM_LIMIT = 56 * 1024 * 1024


def _cparams(n_axes):
    return pltpu.CompilerParams(
        dimension_semantics=("arbitrary",) * n_axes,
        vmem_limit_bytes=VMEM_LIMIT,
    )


def _rms_mod(x, g, shift, scale):
    y = x * lax.rsqrt(jnp.mean(x * x, axis=-1, keepdims=True) + EPS)
    return (y * g) * (1.0 + scale) + shift


def _dot(a, b):
    return jnp.dot(a, b, preferred_element_type=F32)


def _dot_nt(a, b):
    return lax.dot_general(a, b, (((1,), (1,)), ((), ())), preferred_element_type=F32)


def _mod_kernel(c_ref, w_ref, b_ref, o_ref):
    c = c_ref[...]
    s = c * jax.nn.sigmoid(c)
    o_ref[0] = _dot(s.astype(BF16), w_ref[0].astype(BF16)) + b_ref[0]


def _modulation(cc, w_mod, b_mod):
    depth, d, n = w_mod.shape
    rows = cc.shape[0]
    tn = 1536
    return pl.pallas_call(
        _mod_kernel,
        out_shape=jax.ShapeDtypeStruct((depth, rows, n), F32),
        grid=(depth, n // tn),
        in_specs=[
            pl.BlockSpec((rows, d), lambda l, j: (0, 0)),
            pl.BlockSpec((1, d, tn), lambda l, j: (l, 0, j)),
            pl.BlockSpec((1, 1, tn), lambda l, j: (l, 0, j)),
        ],
        out_specs=pl.BlockSpec((1, rows, tn), lambda l, j: (l, 0, j)),
        compiler_params=_cparams(2),
        name="modulation",
    )(cc, w_mod, b_mod.reshape(depth, 1, n))


def _head_rms(t, bd, g):
    sq = t * t
    hi = sq.astype(BF16)
    lo = (sq - hi.astype(F32)).astype(BF16)
    w = t.shape[1]
    parts = []
    for j in range(0, w, 2 * LANES):
        e = min(j + 2 * LANES, w)
        b = bd[: e - j, : e - j]
        parts.append(_dot(hi[:, j:e], b) + _dot(lo[:, j:e], b))
    ssq = parts[0] if len(parts) == 1 else jnp.concatenate(parts, axis=1)
    return t * lax.rsqrt(ssq * (1.0 / HEAD_DIM) + EPS) * g


def _rope(t, cos, sin, first_half):
    outs = []
    for j in range(0, t.shape[1], LANES):
        tj = t[:, j:j + LANES]
        up = pltpu.roll(tj, LANES - 16, axis=1)
        dn = pltpu.roll(tj, 16, axis=1)
        partner = jnp.where(first_half, up, dn)
        outs.append(tj * cos + partner * sin)
    return outs[0] if len(outs) == 1 else jnp.concatenate(outs, axis=1)


def _inproj_kernel(x_ref, mod_ref, g1_ref, w_ref, cos_ref, sin_ref, qg_ref, kg_ref, bd_ref,
                   u_ref, wq_ref, wk_ref, wv_ref, gq_ref, gk_ref, gv_ref, *, rope, pool_w, attn_w, kv_w):
    x = x_ref[0]
    h = _rms_mod(x, g1_ref[...], mod_ref[0, 0:1, :], mod_ref[0, 1:2, :])
    z = _dot(h.astype(BF16), w_ref[...])
    o = 0
    u_ref[0] = z[:, o:o + pool_w]; o += pool_w
    wq = z[:, o:o + attn_w]; o += attn_w
    wk = z[:, o:o + kv_w]; o += kv_w
    wv = z[:, o:o + kv_w]; o += kv_w
    gq = z[:, o:o + attn_w]; o += attn_w
    gk = z[:, o:o + kv_w]; o += kv_w
    gv = z[:, o:o + kv_w]
    bd = bd_ref[...]
    gq = _head_rms(gq, bd, qg_ref[...])
    gk = _head_rms(gk, bd, kg_ref[...])
    if rope:
        cos = cos_ref[...]
        sin = sin_ref[...]
        lane = lax.broadcasted_iota(jnp.int32, cos.shape, 1)
        first_half = (lane % 32) < 16
        wq = _rope(wq, cos, sin, first_half)
        wk = _rope(wk, cos, sin, first_half)
        gq = _rope(gq, cos, sin, first_half)
        gk = _rope(gk, cos, sin, first_half)
    wq_ref[0] = (wq * SM_SCALE).astype(BF16)
    gq_ref[0] = (gq * SM_SCALE).astype(BF16)
    wv_ref[0] = wv.astype(BF16)
    wk_ref[0] = wk.astype(BF16)
    n_kv = kv_w // HEAD_DIM
    for k in range(n_kv):
        gk_ref[0, k] = gk[:, k * HEAD_DIM:(k + 1) * HEAD_DIM].astype(BF16)
        gv_ref[0, k] = gv[:, k * HEAD_DIM:(k + 1) * HEAD_DIM].astype(BF16)


def _inproj(x, mod, g1, w_in, cos, sin, qg, kg, bd, *, rope, tl):
    bn, ln, d = x.shape
    in_w = w_in.shape[1]
    attn_w = qg.shape[1]
    kv_w = kg.shape[1]
    pool_w = in_w - 2 * attn_w - 4 * kv_w
    n_kv = kv_w // HEAD_DIM
    mod_b = mod.shape[0]
    mod_map = (lambda b, i: (b, 0, 0)) if mod_b > 1 else (lambda b, i: (0, 0, 0))
    tok = lambda w: pl.BlockSpec((1, tl, w), lambda b, i: (b, i, 0))
    full2 = lambda a: pl.BlockSpec(a.shape, lambda b, i: (0, 0))
    kern = functools.partial(_inproj_kernel, rope=rope, pool_w=pool_w, attn_w=attn_w, kv_w=kv_w)
    out_shapes = (
        jax.ShapeDtypeStruct((bn, ln, pool_w), F32),
        jax.ShapeDtypeStruct((bn, ln, attn_w), BF16),
        jax.ShapeDtypeStruct((bn, ln, kv_w), BF16),
        jax.ShapeDtypeStruct((bn, ln, kv_w), BF16),
        jax.ShapeDtypeStruct((bn, ln, attn_w), BF16),
        jax.ShapeDtypeStruct((bn, n_kv, ln, HEAD_DIM), BF16),
        jax.ShapeDtypeStruct((bn, n_kv, ln, HEAD_DIM), BF16),
    )
    head_spec = pl.BlockSpec((1, n_kv, tl, HEAD_DIM), lambda b, i: (b, 0, i, 0))
    return pl.pallas_call(
        kern,
        out_shape=out_shapes,
        grid=(bn, ln // tl),
        in_specs=[
            tok(d),
            pl.BlockSpec((1, 6, d), mod_map),
            full2(g1),
            full2(w_in),
            pl.BlockSpec((tl, LANES), lambda b, i: (i, 0)),
            pl.BlockSpec((tl, LANES), lambda b, i: (i, 0)),
            full2(qg),
            full2(kg),
            full2(bd),
        ],
        out_specs=(tok(pool_w), tok(attn_w), tok(kv_w), tok(kv_w), tok(attn_w), head_spec, head_spec),
        compiler_params=_cparams(2),
        name="inproj_rope" if rope else "inproj_ctx",
    )(x, mod, g1, w_in, cos, sin, qg, kg, bd)


def _pool_kernel(u_ref, w_ref, s_ref, o_ref, ext_ref, *, ln):
    n_grp = len(POOL_WINDOWS)
    gw = u_ref.shape[2] // n_grp
    t = lax.broadcasted_iota(jnp.int32, (ln, 1), 0)
    ext_len = ln + 2 * POOL_HALO
    zeros = jnp.zeros((POOL_HALO, gw), F32)
    for gi, win in enumerate(POOL_WINDOWS):
        u = u_ref[0, :, gi * gw:(gi + 1) * gw]
        ext_ref[0:POOL_HALO, :] = zeros
        ext_ref[POOL_HALO + ln:ext_len, :] = zeros
        ext_ref[POOL_HALO:POOL_HALO + ln, :] = u
        e = ext_ref[...]
        acc = e + pltpu.roll(e, 1, axis=0)
        half = 1
        while 2 * half < win:
            acc = pltpu.roll(acc, half, axis=0) + pltpu.roll(acc, ext_len - half, axis=0)
            half *= 2
        wsum = acc[POOL_HALO:POOL_HALO + ln, :]
        cnt = (jnp.minimum(t + win // 2, ln) - jnp.maximum(t - win // 2, 0)).astype(F32)
        p = wsum / cnt - u
        y = _dot(p.astype(BF16), w_ref[gi]) * s_ref[:, gi * gw:(gi + 1) * gw]
        o_ref[0, :, gi * gw:(gi + 1) * gw] = y.astype(BF16)


def _pool(u, w_grp, scale):
    bn, ln, pw = u.shape
    gw = pw // len(POOL_WINDOWS)
    return pl.pallas_call(
        functools.partial(_pool_kernel, ln=ln),
        out_shape=jax.ShapeDtypeStruct((bn, ln, pw), BF16),
        grid=(bn,),
        in_specs=[
            pl.BlockSpec((1, ln, pw), lambda b: (b, 0, 0)),
            pl.BlockSpec(w_grp.shape, lambda b: (0, 0, 0)),
            pl.BlockSpec(scale.shape, lambda b: (0, 0)),
        ],
        out_specs=pl.BlockSpec((1, ln, pw), lambda b: (b, 0, 0)),
        scratch_shapes=[pltpu.VMEM((ln + 2 * POOL_HALO, gw), F32)],
        compiler_params=_cparams(1),
        name="pool_mixer",
    )(u, w_grp, scale)


def _softmax_pv(scores, values, extra_logit=None):
    m = scores[0].max(axis=-1, keepdims=True)
    for s in scores[1:]:
        m = jnp.maximum(m, s.max(axis=-1, keepdims=True))
    if extra_logit is not None:
        m = jnp.maximum(m, extra_logit)
    l = None
    o = None
    for s, v in zip(scores, values):
        p = jnp.exp(s - m)
        ls = p.sum(axis=-1, keepdims=True)
        os_ = _dot(p.astype(BF16), v)
        l = ls if l is None else l + ls
        o = os_ if o is None else o + os_
    if extra_logit is not None:
        l = l + jnp.exp(extra_logit - m)
    return o / l


def _win_kernel(sink_ref, q_ref, kp_ref, kc_ref, kn_ref, vp_ref, vc_ref, vn_ref, kx_ref, vx_ref, o_ref):
    i = pl.program_id(1)
    nb = pl.num_programs(1)
    q = q_ref[0]
    kwin = jnp.concatenate([kp_ref[0], kc_ref[0], kn_ref[0]], axis=0)
    vwin = jnp.concatenate([vp_ref[0], vc_ref[0], vn_ref[0]], axis=0)
    kx = kx_ref[0]
    vx = vx_ref[0]
    row = lax.broadcasted_iota(jnp.int32, (BLOCK, 3 * BLOCK), 0)
    col = lax.broadcasted_iota(jnp.int32, (BLOCK, 3 * BLOCK), 1)
    rel = col - BLOCK - row
    valid = (jnp.abs(rel) <= WINDOW) & ((col >= BLOCK) | (i > 0)) & ((col < 2 * BLOCK) | (i < nb - 1))
    n_heads = q.shape[1] // HEAD_DIM
    outs = []
    for h in range(n_heads):
        kv = h // Q_PER_KV
        sl = slice(kv * HEAD_DIM, (kv + 1) * HEAD_DIM)
        qh = q[:, h * HEAD_DIM:(h + 1) * HEAD_DIM]
        s_lat = jnp.where(valid, _dot_nt(qh, kwin[:, sl]), NEG)
        s_ctx = _dot_nt(qh, kx[:, sl])
        outs.append(_softmax_pv([s_lat, s_ctx], [vwin[:, sl], vx[:, sl]], sink_ref[h]))
    o_ref[0] = jnp.concatenate(outs, axis=1).astype(BF16)


def _window_attention(sink, q, k, v, kx, vx):
    bn, ln, aw = q.shape
    kw = k.shape[2]
    cn = kx.shape[1]
    nb = ln // BLOCK
    cur = lambda b, i: (b, i, 0)
    prev = lambda b, i: (b, jnp.maximum(i - 1, 0), 0)
    nxt = lambda b, i: (b, jnp.minimum(i + 1, nb - 1), 0)
    kvb = lambda f: pl.BlockSpec((1, BLOCK, kw), f)
    ctxb = pl.BlockSpec((1, cn, kw), lambda b, i: (b, 0, 0))
    return pl.pallas_call(
        _win_kernel,
        out_shape=jax.ShapeDtypeStruct((bn, ln, aw), BF16),
        grid=(bn, nb),
        in_specs=[
            pl.BlockSpec(memory_space=pltpu.SMEM),
            pl.BlockSpec((1, BLOCK, aw), cur),
            kvb(prev), kvb(cur), kvb(nxt),
            kvb(prev), kvb(cur), kvb(nxt),
            ctxb, ctxb,
        ],
        out_specs=pl.BlockSpec((1, BLOCK, aw), cur),
        compiler_params=_cparams(2),
        name="window_attention",
    )(sink, q, k, k, k, v, v, v, kx, vx)


def _glob_kernel(q_ref, k_ref, v_ref, kx_ref, vx_ref, o_ref):
    q = q_ref[0]
    k = k_ref[0, 0]
    v = v_ref[0, 0]
    kx = kx_ref[0, 0]
    vx = vx_ref[0, 0]
    outs = []
    for g in range(q.shape[1] // HEAD_DIM):
        qh = q[:, g * HEAD_DIM:(g + 1) * HEAD_DIM]
        outs.append(_softmax_pv([_dot_nt(qh, k), _dot_nt(qh, kx)], [v, vx]))
    o_ref[0] = jnp.concatenate(outs, axis=1).astype(BF16)


def _global_attention(q, k, v, kx, vx, *, tq):
    bn, ln, aw = q.shape
    n_kv = k.shape[1]
    cn = kx.shape[2]
    gw = aw // n_kv
    kvspec = lambda n: pl.BlockSpec((1, 1, n, HEAD_DIM), lambda b, h, i: (b, h, 0, 0))
    return pl.pallas_call(
        _glob_kernel,
        out_shape=jax.ShapeDtypeStruct((bn, ln, aw), BF16),
        grid=(bn, n_kv, ln // tq),
        in_specs=[
            pl.BlockSpec((1, tq, gw), lambda b, h, i: (b, i, h)),
            kvspec(ln), kvspec(ln), kvspec(cn), kvspec(cn),
        ],
        out_specs=pl.BlockSpec((1, tq, gw), lambda b, h, i: (b, i, h)),
        compiler_params=_cparams(3),
        name="global_attention",
    )(q, k, v, kx, vx)


def _ctx_attn_kernel(sink_ref, wq_ref, wk_ref, wv_ref, gq_ref, gk_ref, gv_ref, ow_ref, og_ref):
    wq = wq_ref[0]
    gq = gq_ref[0]
    wk = wk_ref[0]
    wv = wv_ref[0]
    n_heads = wq.shape[1] // HEAD_DIM
    outs_w = []
    outs_g = []
    for h in range(n_heads):
        kv = h // Q_PER_KV
        hs = slice(h * HEAD_DIM, (h + 1) * HEAD_DIM)
        sl = slice(kv * HEAD_DIM, (kv + 1) * HEAD_DIM)
        outs_w.append(_softmax_pv([_dot_nt(wq[:, hs], wk[:, sl])], [wv[:, sl]], sink_ref[h]))
        outs_g.append(_softmax_pv([_dot_nt(gq[:, hs], gk_ref[0, kv])], [gv_ref[0, kv]]))
    ow_ref[0] = jnp.concatenate(outs_w, axis=1).astype(BF16)
    og_ref[0] = jnp.concatenate(outs_g, axis=1).astype(BF16)


def _context_attention(sink, wq, wk, wv, gq, gk, gv):
    bn, cn, aw = wq.shape
    kw = wk.shape[2]
    n_kv = gk.shape[1]
    tok = lambda w: pl.BlockSpec((1, cn, w), lambda b: (b, 0, 0))
    hd = pl.BlockSpec((1, n_kv, cn, HEAD_DIM), lambda b: (b, 0, 0, 0))
    shp = jax.ShapeDtypeStruct((bn, cn, aw), BF16)
    return pl.pallas_call(
        _ctx_attn_kernel,
        out_shape=(shp, shp),
        grid=(bn,),
        in_specs=[pl.BlockSpec(memory_space=pltpu.SMEM), tok(aw), tok(kw), tok(kw), tok(aw), hd, hd],
        out_specs=(tok(aw), tok(aw)),
        compiler_params=_cparams(1),
        name="context_attention",
    )(sink, wq, wk, wv, gq, gk, gv)


def _merge_kernel(x_ref, mod_ref, g1_ref, g2_ref, yp_ref, yw_ref, yg_ref, wg_ref, bg_ref, wb_ref, wo_ref,
                  x1_ref, h2_ref):
    x = x_ref[0]
    h = _rms_mod(x, g1_ref[...], mod_ref[0, 0:1, :], mod_ref[0, 1:2, :]).astype(BF16)
    merged = None
    for i, y_ref in enumerate((yp_ref, yw_ref, yg_ref)):
        gate = jax.nn.sigmoid(_dot(h, wg_ref[i]) + bg_ref[i:i + 1, :])
        term = gate * _dot(y_ref[0], wb_ref[i])
        merged = term if merged is None else merged + term
    x1 = x + mod_ref[0, 2:3, :] * _dot(merged.astype(BF16), wo_ref[...])
    x1_ref[0] = x1
    h2_ref[0] = _rms_mod(x1, g2_ref[...], mod_ref[0, 3:4, :], mod_ref[0, 4:5, :]).astype(BF16)


def _merge(x, mod, g1, g2, yp, yw, yg, wg, bg, wb, wo, *, tl):
    bn, ln, d = x.shape
    bw = yp.shape[2]
    mod_map = (lambda b, i: (b, 0, 0)) if mod.shape[0] > 1 else (lambda b, i: (0, 0, 0))
    tok = lambda w: pl.BlockSpec((1, tl, w), lambda b, i: (b, i, 0))
    full = lambda a: pl.BlockSpec(a.shape, lambda b, i: (0,) * a.ndim)
    return pl.pallas_call(
        _merge_kernel,
        out_shape=(jax.ShapeDtypeStruct((bn, ln, d), F32), jax.ShapeDtypeStruct((bn, ln, d), BF16)),
        grid=(bn, ln // tl),
        in_specs=[tok(d), pl.BlockSpec((1, 6, d), mod_map), full(g1), full(g2), tok(bw), tok(bw), tok(bw),
                  full(wg), full(bg), full(wb), full(wo)],
        out_specs=(tok(d), tok(d)),
        compiler_params=_cparams(2),
        name="merge_branches",
    )(x, mod, g1, g2, yp, yw, yg, wg, bg, wb, wo)


def _ffn_kernel(h_ref, hp_ref, hn_ref, x_ref, mod_ref, wg_ref, wv_ref, cw_ref, cb_ref, wd_ref, fg_ref,
                o_ref, g_scr, *, final):
    j = pl.program_id(1)
    nj = pl.num_programs(1)
    tl = h_ref.shape[1]
    halo = hp_ref.shape[1]
    hm = h_ref[0]
    hp = jnp.where(j > 0, hp_ref[0], jnp.zeros_like(hp_ref[0]))
    hn = jnp.where(j < nj - 1, hn_ref[0], jnp.zeros_like(hn_ref[0]))
    hext = jnp.concatenate([hp, hm, hn], axis=0)
    g_scr[...] = _dot(hext, wg_ref[...])
    a = (g_scr[pl.ds(halo - 1, tl), :] * cw_ref[0:1, :]
         + g_scr[pl.ds(halo, tl), :] * cw_ref[1:2, :]
         + g_scr[pl.ds(halo + 1, tl), :] * cw_ref[2:3, :]
         + cb_ref[...])
    val = _dot(hm, wv_ref[...])
    act = (a * jax.nn.sigmoid(a)) * val
    x2 = x_ref[0] + mod_ref[0, 5:6, :] * _dot(act.astype(BF16), wd_ref[...])
    if final:
        x2 = x2 * lax.rsqrt(jnp.mean(x2 * x2, axis=-1, keepdims=True) + EPS) * fg_ref[...]
    o_ref[0] = x2


def _ffn(h2, x1, mod, wg, wv, cw, cb, wd, fg, *, tl, final):
    bn, ln, d = x1.shape
    f = wg.shape[1]
    halo = BF16_SUBLANES
    r = tl // halo
    nh = ln // halo
    mod_map = (lambda b, i: (b, 0, 0)) if mod.shape[0] > 1 else (lambda b, i: (0, 0, 0))
    tok = pl.BlockSpec((1, tl, d), lambda b, i: (b, i, 0))
    full = lambda a: pl.BlockSpec(a.shape, lambda b, i: (0,) * a.ndim)
    return pl.pallas_call(
        functools.partial(_ffn_kernel, final=final),
        out_shape=jax.ShapeDtypeStruct((bn, ln, d), F32),
        grid=(bn, ln // tl),
        in_specs=[
            tok,
            pl.BlockSpec((1, halo, d), lambda b, i: (b, jnp.maximum(i * r - 1, 0), 0)),
            pl.BlockSpec((1, halo, d), lambda b, i: (b, jnp.minimum((i + 1) * r, nh - 1), 0)),
            tok,
            pl.BlockSpec((1, 6, d), mod_map),
            full(wg), full(wv), full(cw), full(cb), full(wd), full(fg),
        ],
        out_specs=tok,
        scratch_shapes=[pltpu.VMEM((tl + 2 * halo, f), F32)],
        compiler_params=_cparams(2),
        name="conv_glu_final" if final else "conv_glu",
    )(h2, h2, h2, x1, mod, wg, wv, cw, cb, wd, fg)


def _rope_tables(n_tok):
    pos = jnp.arange(n_tok)
    row = (pos // GRID_W).astype(F32)
    col = (pos % GRID_W).astype(F32)
    half = HEAD_DIM // 2
    inv = ROPE_THETA ** (-jnp.arange(0, half, 2, dtype=F32) / half)
    ang_r = row[:, None] * inv
    ang_c = col[:, None] * inv
    cos64 = jnp.concatenate([jnp.cos(ang_r)] * 2 + [jnp.cos(ang_c)] * 2, axis=1)
    sin64 = jnp.concatenate([-jnp.sin(ang_r), jnp.sin(ang_r), -jnp.sin(ang_c), jnp.sin(ang_c)], axis=1)
    reps = LANES // HEAD_DIM
    return jnp.tile(cos64, (1, reps)), jnp.tile(sin64, (1, reps))


def _tile_rows(n, target):
    t = min(n, target)
    while n % t:
        t //= 2
    return t


def kernel(x, c, ctx, c_ctx, w_mod, b_mod, norm1_g, norm2_g, w_in, w_pool_grp, pool_scale, win_sink,
           q_norm_g, k_norm_g, w_branch, w_gate, b_gate, w_out, w_ff_gate, w_ff_val, conv_w, conv_b,
           w_ff_down, final_g):
    bn, ln, d = x.shape
    cn = ctx.shape[1]
    depth = w_mod.shape[0]
    attn_w = d // 2
    kv_w = attn_w // Q_PER_KV
    n_heads = attn_w // HEAD_DIM

    mod_all = _modulation(jnp.concatenate([c, c_ctx[None]], axis=0), w_mod, b_mod)
    mod_all = mod_all.reshape(depth, bn + 1, 6, d)

    cos, sin = _rope_tables(ln)
    cos_c = jnp.ones((cn, LANES), F32)
    sin_c = jnp.zeros((cn, LANES), F32)
    ids = np.arange(2 * LANES) // HEAD_DIM
    bd = jnp.asarray(ids[:, None] == ids[None, :], dtype=BF16)
    fg = final_g.reshape(1, d)

    tl = _tile_rows(ln, 512)
    tl_ffn = _tile_rows(ln, 256)
    tl_c = _tile_rows(cn, 256)
    tq = _tile_rows(ln, 128)

    for l in range(depth):
        last = l == depth - 1
        mod_l = mod_all[l, :bn]
        mod_c = mod_all[l, bn:]
        g1 = norm1_g[l].reshape(1, d)
        g2 = norm2_g[l].reshape(1, d)
        w_in_l = w_in[l].astype(BF16)
        qg = jnp.tile(q_norm_g[l], attn_w // HEAD_DIM).reshape(1, attn_w)
        kg = jnp.tile(k_norm_g[l], kv_w // HEAD_DIM).reshape(1, kv_w)
        w_grp = w_pool_grp[l].astype(BF16)
        p_scale = pool_scale[l].reshape(1, -1)
        sink = win_sink[l]
        wg = w_gate[l].astype(BF16)
        bg = b_gate[l]
        wb = w_branch[l].astype(BF16)
        wo = w_out[l].astype(BF16)
        wfg = w_ff_gate[l].astype(BF16)
        wfv = w_ff_val[l].astype(BF16)
        wfd = w_ff_down[l].astype(BF16)
        cw = conv_w[l]
        cb = conv_b[l].reshape(1, -1)

        u_l, wq_l, wk_l, wv_l, gq_l, gk_l, gv_l = _inproj(
            x, mod_l, g1, w_in_l, cos, sin, qg, kg, bd, rope=True, tl=tl)
        u_c, wq_c, wk_c, wv_c, gq_c, gk_c, gv_c = _inproj(
            ctx, mod_c, g1, w_in_l, cos_c, sin_c, qg, kg, bd, rope=False, tl=tl_c)

        y_pool = _pool(u_l, w_grp, p_scale)
        y_win = _window_attention(sink, wq_l, wk_l, wv_l, wk_c, wv_c)
        y_glob = _global_attention(gq_l, gk_l, gv_l, gk_c, gv_c, tq=tq)
        x1, h2 = _merge(x, mod_l, g1, g2, y_pool, y_win, y_glob, wg, bg, wb, wo, tl=tl)

        if not last:
            yc_pool = _pool(u_c, w_grp, p_scale)
            yc_win, yc_glob = _context_attention(sink, wq_c, wk_c, wv_c, gq_c, gk_c, gv_c)
            c1, hc2 = _merge(ctx, mod_c, g1, g2, yc_pool, yc_win, yc_glob, wg, bg, wb, wo, tl=tl_c)
            ctx = _ffn(hc2, c1, mod_c, wfg, wfv, cw, cb, wfd, fg, tl=tl_c, final=False)

        x = _ffn(h2, x1, mod_l, wfg, wfv, cw, cb, wfd, fg, tl=tl_ffn, final=last)
    return x
```

```python
import functools

import jax
import jax.numpy as jnp
import numpy as np
from jax import lax
from jax.experimental import pallas as pl
from jax.experimental.pallas import tpu as pltpu

F32 = jnp.float32
BF16 = jnp.bfloat16

GRID_W = 64
HEAD_DIM = 64
Q_PER_KV = 4
POOL_WINDOWS = (2, 4, 8, 16)
POOL_HALO = 16
WINDOW = 128
BLOCK = 128
ROPE_THETA = 10000.0
EPS = 1e-6
NEG = -1e30
SM_SCALE = HEAD_DIM ** -0.5
LOG2E = 1.4426950408889634
Q_SCALE = SM_SCALE * LOG2E
BOUND_SLACK = 1.02
BOUND_SHIFT = 60.0
BOUND_MAX = 80.0
LANES = 128
BF16_SUBLANES = 16
VMEM_LIMIT = 56 * 1024 * 1024


def _cparams(n_axes):
    return pltpu.CompilerParams(
        dimension_semantics=("arbitrary",) * n_axes,
        vmem_limit_bytes=VMEM_LIMIT,
    )


def _rms_mod(x, g, shift, scale):
    y = x * lax.rsqrt(jnp.mean(x * x, axis=-1, keepdims=True) + EPS)
    return (y * g) * (1.0 + scale) + shift


def _dot(a, b):
    return jnp.dot(a, b, preferred_element_type=F32)


def _dot_nt(a, b):
    return lax.dot_general(a, b, (((1,), (1,)), ((), ())), preferred_element_type=F32)


def _mod_kernel(c_ref, w_ref, b_ref, o_ref):
    c = c_ref[...]
    s = c * jax.nn.sigmoid(c)
    o_ref[0] = _dot(s.astype(BF16), w_ref[0].astype(BF16)) + b_ref[0]


def _modulation(cc, w_mod, b_mod):
    depth, d, n = w_mod.shape
    rows = cc.shape[0]
    tn = 1536
    return pl.pallas_call(
        _mod_kernel,
        out_shape=jax.ShapeDtypeStruct((depth, rows, n), F32),
        grid=(depth, n // tn),
        in_specs=[
            pl.BlockSpec((rows, d), lambda l, j: (0, 0)),
            pl.BlockSpec((1, d, tn), lambda l, j: (l, 0, j)),
            pl.BlockSpec((1, 1, tn), lambda l, j: (l, 0, j)),
        ],
        out_specs=pl.BlockSpec((1, rows, tn), lambda l, j: (l, 0, j)),
        compiler_params=_cparams(2),
        name="modulation",
    )(cc, w_mod, b_mod.reshape(depth, 1, n))


def _head_rms(t, bd, g):
    sq = t * t
    hi = sq.astype(BF16)
    lo = (sq - hi.astype(F32)).astype(BF16)
    w = t.shape[1]
    parts = []
    for j in range(0, w, 2 * LANES):
        e = min(j + 2 * LANES, w)
        b = bd[: e - j, : e - j]
        parts.append(_dot(hi[:, j:e], b) + _dot(lo[:, j:e], b))
    ssq = parts[0] if len(parts) == 1 else jnp.concatenate(parts, axis=1)
    return t * lax.rsqrt(ssq * (1.0 / HEAD_DIM) + EPS) * g


def _rope(t, cos, sin, first_half):
    outs = []
    for j in range(0, t.shape[1], LANES):
        tj = t[:, j:j + LANES]
        up = pltpu.roll(tj, LANES - 16, axis=1)
        dn = pltpu.roll(tj, 16, axis=1)
        partner = jnp.where(first_half, up, dn)
        outs.append(tj * cos + partner * sin)
    return outs[0] if len(outs) == 1 else jnp.concatenate(outs, axis=1)


def _inproj_kernel(x_ref, mod_ref, g1_ref, w_ref, cos_ref, sin_ref, qg_ref, kg_ref, bd_ref,
                   u_ref, wq_ref, wk_ref, wv_ref, gq_ref, gk_ref, gv_ref, *, rope, pool_w, attn_w, kv_w):
    x = x_ref[0]
    h = _rms_mod(x, g1_ref[...], mod_ref[0, 0:1, :], mod_ref[0, 1:2, :])
    z = _dot(h.astype(BF16), w_ref[...])
    o = 0
    u_ref[0] = z[:, o:o + pool_w]; o += pool_w
    wq = z[:, o:o + attn_w]; o += attn_w
    wk = z[:, o:o + kv_w]; o += kv_w
    wv = z[:, o:o + kv_w]; o += kv_w
    gq = z[:, o:o + attn_w]; o += attn_w
    gk = z[:, o:o + kv_w]; o += kv_w
    gv = z[:, o:o + kv_w]
    bd = bd_ref[...]
    gq = _head_rms(gq, bd, qg_ref[...])
    gk = _head_rms(gk, bd, kg_ref[...])
    if rope:
        cos = cos_ref[...]
        sin = sin_ref[...]
        lane = lax.broadcasted_iota(jnp.int32, cos.shape, 1)
        first_half = (lane % 32) < 16
        wq = _rope(wq, cos, sin, first_half)
        wk = _rope(wk, cos, sin, first_half)
        gq = _rope(gq, cos, sin, first_half)
        gk = _rope(gk, cos, sin, first_half)
    wq_ref[0] = (wq * Q_SCALE).astype(BF16)
    gq_ref[0] = (gq * Q_SCALE).astype(BF16)
    wv_ref[0] = wv.astype(BF16)
    wk_ref[0] = wk.astype(BF16)
    lane_v = lax.broadcasted_iota(jnp.int32, gv.shape, 1)
    ones_col = jnp.where(lane_v == HEAD_DIM, 1.0, 0.0)
    for k in range(kv_w // HEAD_DIM):
        gk_ref[0, k] = gk[:, k * HEAD_DIM:(k + 1) * HEAD_DIM].astype(BF16)
        gvk = gv if k == 0 else pltpu.roll(gv, kv_w - k * HEAD_DIM, axis=1)
        gv_ref[0, k] = jnp.where(lane_v < HEAD_DIM, gvk, ones_col).astype(BF16)


def _inproj(x, mod, g1, w_in, cos, sin, qg, kg, bd, *, rope, tl):
    bn, ln, d = x.shape
    in_w = w_in.shape[1]
    attn_w = qg.shape[1]
    kv_w = kg.shape[1]
    pool_w = in_w - 2 * attn_w - 4 * kv_w
    n_kv = kv_w // HEAD_DIM
    mod_b = mod.shape[0]
    mod_map = (lambda b, i: (b, 0, 0)) if mod_b > 1 else (lambda b, i: (0, 0, 0))
    tok = lambda w: pl.BlockSpec((1, tl, w), lambda b, i: (b, i, 0))
    full2 = lambda a: pl.BlockSpec(a.shape, lambda b, i: (0, 0))
    kern = functools.partial(_inproj_kernel, rope=rope, pool_w=pool_w, attn_w=attn_w, kv_w=kv_w)
    out_shapes = (
        jax.ShapeDtypeStruct((bn, ln, pool_w), F32),
        jax.ShapeDtypeStruct((bn, ln, attn_w), BF16),
        jax.ShapeDtypeStruct((bn, ln, kv_w), BF16),
        jax.ShapeDtypeStruct((bn, ln, kv_w), BF16),
        jax.ShapeDtypeStruct((bn, ln, attn_w), BF16),
        jax.ShapeDtypeStruct((bn, n_kv, ln, HEAD_DIM), BF16),
        jax.ShapeDtypeStruct((bn, n_kv, ln, LANES), BF16),
    )
    assert kv_w == LANES
    head_spec = lambda w: pl.BlockSpec((1, n_kv, tl, w), lambda b, i: (b, 0, i, 0))
    return pl.pallas_call(
        kern,
        out_shape=out_shapes,
        grid=(bn, ln // tl),
        in_specs=[
            tok(d),
            pl.BlockSpec((1, 6, d), mod_map),
            full2(g1),
            full2(w_in),
            pl.BlockSpec((tl, LANES), lambda b, i: (i, 0)),
            pl.BlockSpec((tl, LANES), lambda b, i: (i, 0)),
            full2(qg),
            full2(kg),
            full2(bd),
        ],
        out_specs=(tok(pool_w), tok(attn_w), tok(kv_w), tok(kv_w), tok(attn_w), head_spec(HEAD_DIM), head_spec(LANES)),
        compiler_params=_cparams(2),
        name="inproj_rope" if rope else "inproj_ctx",
    )(x, mod, g1, w_in, cos, sin, qg, kg, bd)


def _pool_kernel(u_ref, w_ref, s_ref, o_ref, ext_ref, *, ln):
    n_grp = len(POOL_WINDOWS)
    gw = u_ref.shape[2] // n_grp
    t = lax.broadcasted_iota(jnp.int32, (ln, 1), 0)
    ext_len = ln + 2 * POOL_HALO
    zeros = jnp.zeros((POOL_HALO, gw), F32)
    for gi, win in enumerate(POOL_WINDOWS):
        u = u_ref[0, :, gi * gw:(gi + 1) * gw]
        ext_ref[0:POOL_HALO, :] = zeros
        ext_ref[POOL_HALO + ln:ext_len, :] = zeros
        ext_ref[POOL_HALO:POOL_HALO + ln, :] = u
        e = ext_ref[...]
        acc = e + pltpu.roll(e, 1, axis=0)
        half = 1
        while 2 * half < win:
            acc = pltpu.roll(acc, half, axis=0) + pltpu.roll(acc, ext_len - half, axis=0)
            half *= 2
        wsum = acc[POOL_HALO:POOL_HALO + ln, :]
        cnt = (jnp.minimum(t + win // 2, ln) - jnp.maximum(t - win // 2, 0)).astype(F32)
        p = wsum / cnt - u
        y = _dot(p.astype(BF16), w_ref[gi]) * s_ref[:, gi * gw:(gi + 1) * gw]
        o_ref[0, :, gi * gw:(gi + 1) * gw] = y.astype(BF16)


def _pool(u, w_grp, scale):
    bn, ln, pw = u.shape
    gw = pw // len(POOL_WINDOWS)
    return pl.pallas_call(
        functools.partial(_pool_kernel, ln=ln),
        out_shape=jax.ShapeDtypeStruct((bn, ln, pw), BF16),
        grid=(bn,),
        in_specs=[
            pl.BlockSpec((1, ln, pw), lambda b: (b, 0, 0)),
            pl.BlockSpec(w_grp.shape, lambda b: (0, 0, 0)),
            pl.BlockSpec(scale.shape, lambda b: (0, 0)),
        ],
        out_specs=pl.BlockSpec((1, ln, pw), lambda b: (b, 0, 0)),
        scratch_shapes=[pltpu.VMEM((ln + 2 * POOL_HALO, gw), F32)],
        compiler_params=_cparams(1),
        name="pool_mixer",
    )(u, w_grp, scale)


def _stack_heads(q, kv):
    return jnp.concatenate(
        [q[:, (kv * Q_PER_KV + g) * HEAD_DIM:(kv * Q_PER_KV + g + 1) * HEAD_DIM] for g in range(Q_PER_KV)], axis=0)


def _unstack_heads(o, t):
    return [o[g * t:(g + 1) * t] for g in range(Q_PER_KV)]


def _sink_column(sink_ref, kv, t):
    row = lax.broadcasted_iota(jnp.int32, (Q_PER_KV * t, 1), 0)
    col = jnp.full((Q_PER_KV * t, 1), sink_ref[kv * Q_PER_KV + Q_PER_KV - 1] * LOG2E, F32)
    for g in range(Q_PER_KV - 2, -1, -1):
        col = jnp.where(row < (g + 1) * t, sink_ref[kv * Q_PER_KV + g] * LOG2E, col)
    return col


def _ones_column(n):
    lane = lax.broadcasted_iota(jnp.int32, (n, LANES), 1)
    return jnp.where(lane == 0, 1.0, 0.0).astype(BF16)


def _win_kernel(sink_ref, q_ref, kp_ref, kc_ref, kn_ref, vp_ref, vc_ref, vn_ref, kx_ref, vx_ref, o_ref):
    i = pl.program_id(1)
    nb = pl.num_programs(1)
    q = q_ref[0]
    kwin = jnp.concatenate([kp_ref[0], kc_ref[0], kn_ref[0]], axis=0)
    vwin = jnp.concatenate([vp_ref[0], vc_ref[0], vn_ref[0]], axis=0)
    kx = kx_ref[0]
    vwin_aug = jnp.concatenate([vwin, _ones_column(vwin.shape[0])], axis=1)
    vx_aug = jnp.concatenate([vx_ref[0], _ones_column(kx.shape[0])], axis=1)
    kw = kwin.shape[1]
    row = lax.broadcasted_iota(jnp.int32, (BLOCK, 3 * BLOCK), 0)
    col = lax.broadcasted_iota(jnp.int32, (BLOCK, 3 * BLOCK), 1)
    rel = col - BLOCK - row
    valid = (jnp.abs(rel) <= WINDOW) & ((col >= BLOCK) | (i > 0)) & ((col < 2 * BLOCK) | (i < nb - 1))
    valid = jnp.concatenate([valid] * Q_PER_KV, axis=0)
    outs = []
    for kv in range(kw // HEAD_DIM):
        sl = slice(kv * HEAD_DIM, (kv + 1) * HEAD_DIM)
        q4 = _stack_heads(q, kv)
        s_lat = jnp.where(valid, _dot_nt(q4, kwin[:, sl]), NEG)
        s_ctx = _dot_nt(q4, kx[:, sl])
        sink = _sink_column(sink_ref, kv, BLOCK)
        m = jnp.maximum(jnp.maximum(s_lat.max(axis=-1, keepdims=True), s_ctx.max(axis=-1, keepdims=True)), sink)
        oa = (_dot(jnp.exp2(s_lat - m).astype(BF16), vwin_aug)
              + _dot(jnp.exp2(s_ctx - m).astype(BF16), vx_aug))
        l = oa[:, kw:kw + 1] + jnp.exp2(sink - m)
        outs += _unstack_heads(oa[:, sl] / l, BLOCK)
    o_ref[0] = jnp.concatenate(outs, axis=1).astype(BF16)


def _window_attention(sink, q, k, v, kx, vx):
    bn, ln, aw = q.shape
    kw = k.shape[2]
    cn = kx.shape[1]
    nb = ln // BLOCK
    cur = lambda b, i: (b, i, 0)
    prev = lambda b, i: (b, jnp.maximum(i - 1, 0), 0)
    nxt = lambda b, i: (b, jnp.minimum(i + 1, nb - 1), 0)
    kvb = lambda f: pl.BlockSpec((1, BLOCK, kw), f)
    ctxb = pl.BlockSpec((1, cn, kw), lambda b, i: (b, 0, 0))
    return pl.pallas_call(
        _win_kernel,
        out_shape=jax.ShapeDtypeStruct((bn, ln, aw), BF16),
        grid=(bn, nb),
        in_specs=[
            pl.BlockSpec(memory_space=pltpu.SMEM),
            pl.BlockSpec((1, BLOCK, aw), cur),
            kvb(prev), kvb(cur), kvb(nxt),
            kvb(prev), kvb(cur), kvb(nxt),
            ctxb, ctxb,
        ],
        out_specs=pl.BlockSpec((1, BLOCK, aw), cur),
        compiler_params=_cparams(2),
        name="window_attention",
    )(sink, q, k, k, k, v, v, v, kx, vx)


def _glob_kernel(bound_ref, q_ref, k_ref, v_ref, kx_ref, vx_ref, o_ref, m_ref, *, kc):
    ln = k_ref.shape[2]
    cn = kx_ref.shape[2]
    q = q_ref[0]
    tq = q.shape[0]
    q4 = _stack_heads(q, 0)
    pieces = [(k_ref, v_ref, c0, min(kc, ln - c0)) for c0 in range(0, ln, kc)] + [(kx_ref, vx_ref, 0, cn)]

    bound = bound_ref[0]
    m_ref[...] = jnp.full(m_ref.shape, bound - BOUND_SHIFT, F32)

    @pl.when(bound > BOUND_MAX)
    def _():
        m = None
        for kr, _, c0, n in pieces:
            mc = _dot_nt(q4, kr[0, 0, c0:c0 + n, :]).max(axis=-1, keepdims=True)
            m = mc if m is None else jnp.maximum(m, mc)
        m_ref[...] = m

    m = m_ref[...]
    acc = None
    for kr, vr, c0, n in pieces:
        s = _dot_nt(q4, kr[0, 0, c0:c0 + n, :])
        oc = _dot(jnp.exp2(s - m).astype(BF16), vr[0, 0, c0:c0 + n, :])
        acc = oc if acc is None else acc + oc
    o = acc[:, :HEAD_DIM] / acc[:, HEAD_DIM:HEAD_DIM + 1]
    o_ref[0] = jnp.concatenate(_unstack_heads(o, tq), axis=1).astype(BF16)


def _score_bound(qg, kg):
    return (HEAD_DIM * Q_SCALE * BOUND_SLACK * jnp.max(jnp.abs(qg)) * jnp.max(jnp.abs(kg))).reshape(1)


def _global_attention(bound, q, k, v, kx, vx, *, tq, kc):
    bn, ln, aw = q.shape
    n_kv = k.shape[1]
    cn = kx.shape[2]
    gw = aw // n_kv
    assert gw == Q_PER_KV * HEAD_DIM
    kspec = lambda n: pl.BlockSpec((1, 1, n, HEAD_DIM), lambda b, h, i: (b, h, 0, 0))
    vspec = lambda n: pl.BlockSpec((1, 1, n, LANES), lambda b, h, i: (b, h, 0, 0))
    return pl.pallas_call(
        functools.partial(_glob_kernel, kc=kc),
        out_shape=jax.ShapeDtypeStruct((bn, ln, aw), BF16),
        grid=(bn, n_kv, ln // tq),
        in_specs=[
            pl.BlockSpec(memory_space=pltpu.SMEM),
            pl.BlockSpec((1, tq, gw), lambda b, h, i: (b, i, h)),
            kspec(ln), vspec(ln), kspec(cn), vspec(cn),
        ],
        out_specs=pl.BlockSpec((1, tq, gw), lambda b, h, i: (b, i, h)),
        scratch_shapes=[pltpu.VMEM((Q_PER_KV * tq, 1), F32)],
        compiler_params=_cparams(3),
        name="global_attention",
    )(bound, q, k, v, kx, vx)


def _ctx_attn_kernel(sink_ref, wq_ref, wk_ref, wv_ref, gq_ref, gk_ref, gv_ref, ow_ref, og_ref):
    wq = wq_ref[0]
    gq = gq_ref[0]
    wk = wk_ref[0]
    cn, kw = wk.shape
    wv_aug = jnp.concatenate([wv_ref[0], _ones_column(cn)], axis=1)
    outs_w = []
    outs_g = []
    for kv in range(kw // HEAD_DIM):
        sl = slice(kv * HEAD_DIM, (kv + 1) * HEAD_DIM)
        s = _dot_nt(_stack_heads(wq, kv), wk[:, sl])
        sink = _sink_column(sink_ref, kv, cn)
        m = jnp.maximum(s.max(axis=-1, keepdims=True), sink)
        oa = _dot(jnp.exp2(s - m).astype(BF16), wv_aug)
        l = oa[:, kw:kw + 1] + jnp.exp2(sink - m)
        outs_w += _unstack_heads(oa[:, sl] / l, cn)
        s = _dot_nt(_stack_heads(gq, kv), gk_ref[0, kv])
        m = s.max(axis=-1, keepdims=True)
        oa = _dot(jnp.exp2(s - m).astype(BF16), gv_ref[0, kv])
        outs_g += _unstack_heads(oa[:, :HEAD_DIM] / oa[:, HEAD_DIM:HEAD_DIM + 1], cn)
    ow_ref[0] = jnp.concatenate(outs_w, axis=1).astype(BF16)
    og_ref[0] = jnp.concatenate(outs_g, axis=1).astype(BF16)


def _context_attention(sink, wq, wk, wv, gq, gk, gv):
    bn, cn, aw = wq.shape
    kw = wk.shape[2]
    n_kv = gk.shape[1]
    tok = lambda w: pl.BlockSpec((1, cn, w), lambda b: (b, 0, 0))
    hd = lambda w: pl.BlockSpec((1, n_kv, cn, w), lambda b: (b, 0, 0, 0))
    shp = jax.ShapeDtypeStruct((bn, cn, aw), BF16)
    return pl.pallas_call(
        _ctx_attn_kernel,
        out_shape=(shp, shp),
        grid=(bn,),
        in_specs=[pl.BlockSpec(memory_space=pltpu.SMEM), tok(aw), tok(kw), tok(kw), tok(aw),
                  hd(HEAD_DIM), hd(LANES)],
        out_specs=(tok(aw), tok(aw)),
        compiler_params=_cparams(1),
        name="context_attention",
    )(sink, wq, wk, wv, gq, gk, gv)


def _merge_kernel(x_ref, mod_ref, g1_ref, g2_ref, yp_ref, yw_ref, yg_ref, wg_ref, bg_ref, wb_ref, wo_ref,
                  x1_ref, h2_ref):
    x = x_ref[0]
    h = _rms_mod(x, g1_ref[...], mod_ref[0, 0:1, :], mod_ref[0, 1:2, :]).astype(BF16)
    merged = None
    for i, y_ref in enumerate((yp_ref, yw_ref, yg_ref)):
        gate = jax.nn.sigmoid(_dot(h, wg_ref[i]) + bg_ref[i:i + 1, :])
        term = gate * _dot(y_ref[0], wb_ref[i])
        merged = term if merged is None else merged + term
    x1 = x + mod_ref[0, 2:3, :] * _dot(merged.astype(BF16), wo_ref[...])
    x1_ref[0] = x1
    h2_ref[0] = _rms_mod(x1, g2_ref[...], mod_ref[0, 3:4, :], mod_ref[0, 4:5, :]).astype(BF16)


def _merge(x, mod, g1, g2, yp, yw, yg, wg, bg, wb, wo, *, tl):
    bn, ln, d = x.shape
    bw = yp.shape[2]
    mod_map = (lambda b, i: (b, 0, 0)) if mod.shape[0] > 1 else (lambda b, i: (0, 0, 0))
    tok = lambda w: pl.BlockSpec((1, tl, w), lambda b, i: (b, i, 0))
    full = lambda a: pl.BlockSpec(a.shape, lambda b, i: (0,) * a.ndim)
    return pl.pallas_call(
        _merge_kernel,
        out_shape=(jax.ShapeDtypeStruct((bn, ln, d), F32), jax.ShapeDtypeStruct((bn, ln, d), BF16)),
        grid=(bn, ln // tl),
        in_specs=[tok(d), pl.BlockSpec((1, 6, d), mod_map), full(g1), full(g2), tok(bw), tok(bw), tok(bw),
                  full(wg), full(bg), full(wb), full(wo)],
        out_specs=(tok(d), tok(d)),
        compiler_params=_cparams(2),
        name="merge_branches",
    )(x, mod, g1, g2, yp, yw, yg, wg, bg, wb, wo)


def _ffn_kernel(h_ref, hp_ref, hn_ref, x_ref, mod_ref, wg_ref, wv_ref, cw_ref, cb_ref, wd_ref, fg_ref,
                o_ref, g_scr, *, final):
    j = pl.program_id(1)
    nj = pl.num_programs(1)
    tl = h_ref.shape[1]
    halo = hp_ref.shape[1]
    hm = h_ref[0]
    hp = jnp.where(j > 0, hp_ref[0], jnp.zeros_like(hp_ref[0]))
    hn = jnp.where(j < nj - 1, hn_ref[0], jnp.zeros_like(hn_ref[0]))
    hext = jnp.concatenate([hp, hm, hn], axis=0)
    g_scr[...] = _dot(hext, wg_ref[...])
    a = (g_scr[pl.ds(halo - 1, tl), :] * cw_ref[0:1, :]
         + g_scr[pl.ds(halo, tl), :] * cw_ref[1:2, :]
         + g_scr[pl.ds(halo + 1, tl), :] * cw_ref[2:3, :]
         + cb_ref[...])
    val = _dot(hm, wv_ref[...])
    act = (a * jax.nn.sigmoid(a)) * val
    x2 = x_ref[0] + mod_ref[0, 5:6, :] * _dot(act.astype(BF16), wd_ref[...])
    if final:
        x2 = x2 * lax.rsqrt(jnp.mean(x2 * x2, axis=-1, keepdims=True) + EPS) * fg_ref[...]
    o_ref[0] = x2


def _ffn(h2, x1, mod, wg, wv, cw, cb, wd, fg, *, tl, final):
    bn, ln, d = x1.shape
    f = wg.shape[1]
    halo = BF16_SUBLANES
    r = tl // halo
    nh = ln // halo
    mod_map = (lambda b, i: (b, 0, 0)) if mod.shape[0] > 1 else (lambda b, i: (0, 0, 0))
    tok = pl.BlockSpec((1, tl, d), lambda b, i: (b, i, 0))
    full = lambda a: pl.BlockSpec(a.shape, lambda b, i: (0,) * a.ndim)
    return pl.pallas_call(
        functools.partial(_ffn_kernel, final=final),
        out_shape=jax.ShapeDtypeStruct((bn, ln, d), F32),
        grid=(bn, ln // tl),
        in_specs=[
            tok,
            pl.BlockSpec((1, halo, d), lambda b, i: (b, jnp.maximum(i * r - 1, 0), 0)),
            pl.BlockSpec((1, halo, d), lambda b, i: (b, jnp.minimum((i + 1) * r, nh - 1), 0)),
            tok,
            pl.BlockSpec((1, 6, d), mod_map),
            full(wg), full(wv), full(cw), full(cb), full(wd), full(fg),
        ],
        out_specs=tok,
        scratch_shapes=[pltpu.VMEM((tl + 2 * halo, f), F32)],
        compiler_params=_cparams(2),
        name="conv_glu_final" if final else "conv_glu",
    )(h2, h2, h2, x1, mod, wg, wv, cw, cb, wd, fg)


def _rope_tables(n_tok):
    pos = jnp.arange(n_tok)
    row = (pos // GRID_W).astype(F32)
    col = (pos % GRID_W).astype(F32)
    half = HEAD_DIM // 2
    inv = ROPE_THETA ** (-jnp.arange(0, half, 2, dtype=F32) / half)
    ang_r = row[:, None] * inv
    ang_c = col[:, None] * inv
    cos64 = jnp.concatenate([jnp.cos(ang_r)] * 2 + [jnp.cos(ang_c)] * 2, axis=1)
    sin64 = jnp.concatenate([-jnp.sin(ang_r), jnp.sin(ang_r), -jnp.sin(ang_c), jnp.sin(ang_c)], axis=1)
    reps = LANES // HEAD_DIM
    return jnp.tile(cos64, (1, reps)), jnp.tile(sin64, (1, reps))


def _tile_rows(n, target):
    t = min(n, target)
    while n % t:
        t //= 2
    return t


def kernel(x, c, ctx, c_ctx, w_mod, b_mod, norm1_g, norm2_g, w_in, w_pool_grp, pool_scale, win_sink,
           q_norm_g, k_norm_g, w_branch, w_gate, b_gate, w_out, w_ff_gate, w_ff_val, conv_w, conv_b,
           w_ff_down, final_g):
    bn, ln, d = x.shape
    cn = ctx.shape[1]
    depth = w_mod.shape[0]
    attn_w = d // 2
    kv_w = attn_w // Q_PER_KV
    n_heads = attn_w // HEAD_DIM

    mod_all = _modulation(jnp.concatenate([c, c_ctx[None]], axis=0), w_mod, b_mod)
    mod_all = mod_all.reshape(depth, bn + 1, 6, d)

    cos, sin = _rope_tables(ln)
    cos_c = jnp.ones((cn, LANES), F32)
    sin_c = jnp.zeros((cn, LANES), F32)
    ids = np.arange(2 * LANES) // HEAD_DIM
    bd = jnp.asarray(ids[:, None] == ids[None, :], dtype=BF16)
    fg = final_g.reshape(1, d)

    tl = _tile_rows(ln, 512)
    tl_ffn = _tile_rows(ln, 256)
    tl_c = _tile_rows(cn, 256)
    tq = _tile_rows(ln, 256)

    for l in range(depth):
        last = l == depth - 1
        mod_l = mod_all[l, :bn]
        mod_c = mod_all[l, bn:]
        g1 = norm1_g[l].reshape(1, d)
        g2 = norm2_g[l].reshape(1, d)
        w_in_l = w_in[l].astype(BF16)
        qg = jnp.tile(q_norm_g[l], attn_w // HEAD_DIM).reshape(1, attn_w)
        kg = jnp.tile(k_norm_g[l], kv_w // HEAD_DIM).reshape(1, kv_w)
        w_grp = w_pool_grp[l].astype(BF16)
        p_scale = pool_scale[l].reshape(1, -1)
        sink = win_sink[l]
        wg = w_gate[l].astype(BF16)
        bg = b_gate[l]
        wb = w_branch[l].astype(BF16)
        wo = w_out[l].astype(BF16)
        wfg = w_ff_gate[l].astype(BF16)
        wfv = w_ff_val[l].astype(BF16)
        wfd = w_ff_down[l].astype(BF16)
        cw = conv_w[l]
        cb = conv_b[l].reshape(1, -1)

        u_l, wq_l, wk_l, wv_l, gq_l, gk_l, gv_l = _inproj(
            x, mod_l, g1, w_in_l, cos, sin, qg, kg, bd, rope=True, tl=tl)
        u_c, wq_c, wk_c, wv_c, gq_c, gk_c, gv_c = _inproj(
            ctx, mod_c, g1, w_in_l, cos_c, sin_c, qg, kg, bd, rope=False, tl=tl_c)

        y_pool = _pool(u_l, w_grp, p_scale)
        y_win = _window_attention(sink, wq_l, wk_l, wv_l, wk_c, wv_c)
        bound = _score_bound(q_norm_g[l], k_norm_g[l])
        y_glob = _global_attention(bound, gq_l, gk_l, gv_l, gk_c, gv_c, tq=tq, kc=512)
        x1, h2 = _merge(x, mod_l, g1, g2, y_pool, y_win, y_glob, wg, bg, wb, wo, tl=tl)

        if not last:
            yc_pool = _pool(u_c, w_grp, p_scale)
            yc_win, yc_glob = _context_attention(sink, wq_c, wk_c, wv_c, gq_c, gk_c, gv_c)
            c1, hc2 = _merge(ctx, mod_c, g1, g2, yc_pool, yc_win, yc_glob, wg, bg, wb, wo, tl=tl_c)
            ctx = _ffn(hc2, c1, mod_c, wfg, wfv, cw, cb, wfd, fg, tl=tl_c, final=False)

        x = _ffn(h2, x1, mod_l, wfg, wfv, cw, cb, wfd, fg, tl=tl_ffn, final=last)
    return x
```

```python
import functools

import jax
import jax.numpy as jnp
import numpy as np
from jax import lax
from jax.experimental import pallas as pl
from jax.experimental.pallas import tpu as pltpu

F32 = jnp.float32
BF16 = jnp.bfloat16

GRID_W = 64
HEAD_DIM = 64
Q_PER_KV = 4
POOL_WINDOWS = (2, 4, 8, 16)
POOL_HALO = 16
WINDOW = 128
BLOCK = 128
ROPE_THETA = 10000.0
EPS = 1e-6
NEG = -1e30
SM_SCALE = HEAD_DIM ** -0.5
LOG2E = 1.4426950408889634
Q_SCALE = SM_SCALE * LOG2E
BOUND_SLACK = 1.02
BOUND_SHIFT = 60.0
BOUND_MAX = 80.0
LANES = 128
BF16_SUBLANES = 16
VMEM_LIMIT = 56 * 1024 * 1024


def _cparams(n_axes):
    return pltpu.CompilerParams(
        dimension_semantics=("arbitrary",) * n_axes,
        vmem_limit_bytes=VMEM_LIMIT,
    )


def _resident(a):
    return pl.BlockSpec(a.shape, lambda *_: (0,) * a.ndim, pipeline_mode=pl.Buffered(1))


def _rms_mod(x, g, shift, scale):
    y = x * lax.rsqrt(jnp.mean(x * x, axis=-1, keepdims=True) + EPS)
    return (y * g) * (1.0 + scale) + shift


def _dot(a, b):
    return jnp.dot(a, b, preferred_element_type=F32)


def _dot_nt(a, b):
    return lax.dot_general(a, b, (((1,), (1,)), ((), ())), preferred_element_type=F32)


def _mod_kernel(c_ref, w_ref, b_ref, o_ref):
    c = c_ref[...]
    s = c * jax.nn.sigmoid(c)
    o_ref[0] = _dot(s.astype(BF16), w_ref[0].astype(BF16)) + b_ref[0]


def _modulation(cc, w_mod, b_mod):
    depth, d, n = w_mod.shape
    rows = cc.shape[0]
    tn = 1536
    return pl.pallas_call(
        _mod_kernel,
        out_shape=jax.ShapeDtypeStruct((depth, rows, n), F32),
        grid=(depth, n // tn),
        in_specs=[
            pl.BlockSpec((rows, d), lambda l, j: (0, 0)),
            pl.BlockSpec((1, d, tn), lambda l, j: (l, 0, j)),
            pl.BlockSpec((1, 1, tn), lambda l, j: (l, 0, j)),
        ],
        out_specs=pl.BlockSpec((1, rows, tn), lambda l, j: (l, 0, j)),
        compiler_params=_cparams(2),
        name="modulation",
    )(cc, w_mod, b_mod.reshape(depth, 1, n))


def _head_rms(t, bd, g):
    sq = t * t
    hi = sq.astype(BF16)
    lo = (sq - hi.astype(F32)).astype(BF16)
    w = t.shape[1]
    parts = []
    for j in range(0, w, 2 * LANES):
        e = min(j + 2 * LANES, w)
        b = bd[: e - j, : e - j]
        parts.append(_dot(hi[:, j:e], b) + _dot(lo[:, j:e], b))
    ssq = parts[0] if len(parts) == 1 else jnp.concatenate(parts, axis=1)
    return t * lax.rsqrt(ssq * (1.0 / HEAD_DIM) + EPS) * g


def _rope(t, cos, sin, partner_lane):
    outs = []
    for j in range(0, t.shape[1], LANES):
        tj = t[:, j:j + LANES]
        partner = jnp.take_along_axis(tj, partner_lane, axis=1)
        outs.append(tj * cos + partner * sin)
    return outs[0] if len(outs) == 1 else jnp.concatenate(outs, axis=1)


def _inproj_kernel(x_ref, mod_ref, g1_ref, w_ref, cos_ref, sin_ref, qg_ref, kg_ref, bd_ref,
                   u_ref, wq_ref, wk_ref, wv_ref, gq_ref, gk_ref, gv_ref, *, rope, pool_w, attn_w, kv_w):
    x = x_ref[0]
    h = _rms_mod(x, g1_ref[...], mod_ref[0, 0:1, :], mod_ref[0, 1:2, :])
    h = h.astype(BF16)
    mix_w = attn_w + 2 * kv_w
    zg = _dot(h, w_ref[:, pool_w + mix_w:pool_w + 2 * mix_w])
    zw = _dot(h, w_ref[:, pool_w:pool_w + mix_w])
    u_ref[0] = _dot(h, w_ref[:, :pool_w])
    gq, gk, gv = zg[:, :attn_w], zg[:, attn_w:attn_w + kv_w], zg[:, attn_w + kv_w:]
    wq, wk, wv = zw[:, :attn_w], zw[:, attn_w:attn_w + kv_w], zw[:, attn_w + kv_w:]
    bd = bd_ref[...]
    gq = _head_rms(gq, bd, qg_ref[...])
    gk = _head_rms(gk, bd, kg_ref[...])
    if rope:
        cos = cos_ref[...]
        sin = sin_ref[...]
        lane = lax.broadcasted_iota(jnp.int32, cos.shape, 1)
        partner_lane = lane ^ 16
        wq = _rope(wq, cos, sin, partner_lane)
        wk = _rope(wk, cos, sin, partner_lane)
        gq = _rope(gq, cos, sin, partner_lane)
        gk = _rope(gk, cos, sin, partner_lane)
    wq_ref[0] = (wq * Q_SCALE).astype(BF16)
    gq_ref[0] = (gq * Q_SCALE).astype(BF16)
    wv_ref[0] = wv.astype(BF16)
    wk_ref[0] = wk.astype(BF16)
    lane_v = lax.broadcasted_iota(jnp.int32, gv.shape, 1)
    ones_col = jnp.where(lane_v == HEAD_DIM, 1.0, 0.0)
    for k in range(kv_w // HEAD_DIM):
        gk_ref[0, k] = gk[:, k * HEAD_DIM:(k + 1) * HEAD_DIM].astype(BF16)
        gvk = gv if k == 0 else pltpu.roll(gv, kv_w - k * HEAD_DIM, axis=1)
        gv_ref[0, k] = jnp.where(lane_v < HEAD_DIM, gvk, ones_col).astype(BF16)


def _inproj(x, mod, g1, w_in, cos, sin, qg, kg, bd, *, rope, tl):
    bn, ln, d = x.shape
    in_w = w_in.shape[1]
    attn_w = qg.shape[1]
    kv_w = kg.shape[1]
    pool_w = in_w - 2 * attn_w - 4 * kv_w
    n_kv = kv_w // HEAD_DIM
    mod_b = mod.shape[0]
    mod_map = (lambda b, i: (b, 0, 0)) if mod_b > 1 else (lambda b, i: (0, 0, 0))
    tok = lambda w: pl.BlockSpec((1, tl, w), lambda b, i: (b, i, 0))
    full2 = _resident
    kern = functools.partial(_inproj_kernel, rope=rope, pool_w=pool_w, attn_w=attn_w, kv_w=kv_w)
    out_shapes = (
        jax.ShapeDtypeStruct((bn, ln, pool_w), F32),
        jax.ShapeDtypeStruct((bn, ln, attn_w), BF16),
        jax.ShapeDtypeStruct((bn, ln, kv_w), BF16),
        jax.ShapeDtypeStruct((bn, ln, kv_w), BF16),
        jax.ShapeDtypeStruct((bn, ln, attn_w), BF16),
        jax.ShapeDtypeStruct((bn, n_kv, ln, HEAD_DIM), BF16),
        jax.ShapeDtypeStruct((bn, n_kv, ln, LANES), BF16),
    )
    assert kv_w == LANES
    head_spec = lambda w: pl.BlockSpec((1, n_kv, tl, w), lambda b, i: (b, 0, i, 0))
    return pl.pallas_call(
        kern,
        out_shape=out_shapes,
        grid=(bn, ln // tl),
        in_specs=[
            tok(d),
            pl.BlockSpec((1, 6, d), mod_map),
            full2(g1),
            full2(w_in),
            pl.BlockSpec((tl, LANES), lambda b, i: (i, 0)),
            pl.BlockSpec((tl, LANES), lambda b, i: (i, 0)),
            full2(qg),
            full2(kg),
            full2(bd),
        ],
        out_specs=(tok(pool_w), tok(attn_w), tok(kv_w), tok(kv_w), tok(attn_w), head_spec(HEAD_DIM), head_spec(LANES)),
        compiler_params=_cparams(2),
        name="inproj_rope" if rope else "inproj_ctx",
    )(x, mod, g1, w_in, cos, sin, qg, kg, bd)


def _pool_kernel(u_ref, w_ref, s_ref, o_ref, ext_ref, *, ln):
    n_grp = len(POOL_WINDOWS)
    gw = u_ref.shape[2] // n_grp
    t = lax.broadcasted_iota(jnp.int32, (ln, 1), 0)
    ext_len = ln + 2 * POOL_HALO
    zeros = jnp.zeros((POOL_HALO, gw), F32)
    for gi, win in enumerate(POOL_WINDOWS):
        u = u_ref[0, :, gi * gw:(gi + 1) * gw]
        ext_ref[0:POOL_HALO, :] = zeros
        ext_ref[POOL_HALO + ln:ext_len, :] = zeros
        ext_ref[POOL_HALO:POOL_HALO + ln, :] = u
        e = ext_ref[...]
        acc = e + pltpu.roll(e, 1, axis=0)
        half = 1
        while 2 * half < win:
            acc = pltpu.roll(acc, half, axis=0) + pltpu.roll(acc, ext_len - half, axis=0)
            half *= 2
        wsum = acc[POOL_HALO:POOL_HALO + ln, :]
        cnt = (jnp.minimum(t + win // 2, ln) - jnp.maximum(t - win // 2, 0)).astype(F32)
        p = wsum / cnt - u
        y = _dot(p.astype(BF16), w_ref[gi]) * s_ref[:, gi * gw:(gi + 1) * gw]
        o_ref[0, :, gi * gw:(gi + 1) * gw] = y.astype(BF16)


def _pool(u, w_grp, scale):
    bn, ln, pw = u.shape
    gw = pw // len(POOL_WINDOWS)
    return pl.pallas_call(
        functools.partial(_pool_kernel, ln=ln),
        out_shape=jax.ShapeDtypeStruct((bn, ln, pw), BF16),
        grid=(bn,),
        in_specs=[
            pl.BlockSpec((1, ln, pw), lambda b: (b, 0, 0)),
            pl.BlockSpec(w_grp.shape, lambda b: (0, 0, 0)),
            pl.BlockSpec(scale.shape, lambda b: (0, 0)),
        ],
        out_specs=pl.BlockSpec((1, ln, pw), lambda b: (b, 0, 0)),
        scratch_shapes=[pltpu.VMEM((ln + 2 * POOL_HALO, gw), F32)],
        compiler_params=_cparams(1),
        name="pool_mixer",
    )(u, w_grp, scale)


def _stack_heads(q, kv):
    return jnp.concatenate(
        [q[:, (kv * Q_PER_KV + g) * HEAD_DIM:(kv * Q_PER_KV + g + 1) * HEAD_DIM] for g in range(Q_PER_KV)], axis=0)


def _unstack_heads(o, t):
    return [o[g * t:(g + 1) * t] for g in range(Q_PER_KV)]


def _sink_column(sink_ref, kv, t):
    row = lax.broadcasted_iota(jnp.int32, (Q_PER_KV * t, 1), 0)
    col = jnp.full((Q_PER_KV * t, 1), sink_ref[kv * Q_PER_KV + Q_PER_KV - 1] * LOG2E, F32)
    for g in range(Q_PER_KV - 2, -1, -1):
        col = jnp.where(row < (g + 1) * t, sink_ref[kv * Q_PER_KV + g] * LOG2E, col)
    return col


def _ones_column(n):
    return jnp.ones((n, LANES), BF16)


def _win_kernel(sink_ref, q_ref, kp_ref, kc_ref, kn_ref, vp_ref, vc_ref, vn_ref, kx_ref, vx_ref, o_ref, *, qb):
    i = pl.program_id(1)
    ni = pl.num_programs(1)
    kall = jnp.concatenate([kp_ref[0], kc_ref[0], kn_ref[0]], axis=0)
    vall = jnp.concatenate([vp_ref[0], vc_ref[0], vn_ref[0]], axis=0)
    kx = kx_ref[0]
    vall_aug = jnp.concatenate([vall, _ones_column(vall.shape[0])], axis=1)
    vx_aug = jnp.concatenate([vx_ref[0], _ones_column(kx.shape[0])], axis=1)
    kw = kall.shape[1]
    row = lax.broadcasted_iota(jnp.int32, (BLOCK, 3 * BLOCK), 0)
    col = lax.broadcasted_iota(jnp.int32, (BLOCK, 3 * BLOCK), 1)
    band = jnp.abs(col - BLOCK - row) <= WINDOW
    blocks = []
    for j in range(qb):
        valid = band
        if j == 0:
            valid = valid & ((col >= BLOCK) | (i > 0))
        if j == qb - 1:
            valid = valid & ((col < 2 * BLOCK) | (i < ni - 1))
        valid = jnp.concatenate([valid] * Q_PER_KV, axis=0)
        q = q_ref[0, j * BLOCK:(j + 1) * BLOCK, :]
        kwin = kall[j * BLOCK:(j + 3) * BLOCK]
        vwin_aug = vall_aug[j * BLOCK:(j + 3) * BLOCK]
        outs = []
        for kv in range(kw // HEAD_DIM):
            sl = slice(kv * HEAD_DIM, (kv + 1) * HEAD_DIM)
            q4 = _stack_heads(q, kv)
            s_lat = jnp.where(valid, _dot_nt(q4, kwin[:, sl]), NEG)
            s_ctx = _dot_nt(q4, kx[:, sl])
            sink = _sink_column(sink_ref, kv, BLOCK)
            m = jnp.maximum(jnp.concatenate([s_lat, s_ctx], axis=1).max(axis=-1, keepdims=True), sink)
            oa = (_dot(jnp.exp2(s_lat - m).astype(BF16), vwin_aug)
                  + _dot(jnp.exp2(s_ctx - m).astype(BF16), vx_aug))
            l = oa[:, kw:] + jnp.exp2(sink - m)
            outs += _unstack_heads((oa[:, :kw] / l)[:, sl], BLOCK)
        blocks.append(jnp.concatenate(outs, axis=1).astype(BF16))
    o_ref[0] = blocks[0] if qb == 1 else jnp.concatenate(blocks, axis=0)


def _window_attention(sink, q, k, v, kx, vx, *, qb):
    bn, ln, aw = q.shape
    kw = k.shape[2]
    cn = kx.shape[1]
    nb = ln // BLOCK
    assert kw == LANES and nb % qb == 0
    cur = lambda b, i: (b, i, 0)
    prev = lambda b, i: (b, jnp.maximum(i * qb - 1, 0), 0)
    nxt = lambda b, i: (b, jnp.minimum((i + 1) * qb, nb - 1), 0)
    edge = lambda f: pl.BlockSpec((1, BLOCK, kw), f)
    mid = pl.BlockSpec((1, qb * BLOCK, kw), cur)
    ctxb = pl.BlockSpec((1, cn, kw), lambda b, i: (b, 0, 0))
    return pl.pallas_call(
        functools.partial(_win_kernel, qb=qb),
        out_shape=jax.ShapeDtypeStruct((bn, ln, aw), BF16),
        grid=(bn, nb // qb),
        in_specs=[
            pl.BlockSpec(memory_space=pltpu.SMEM),
            pl.BlockSpec((1, qb * BLOCK, aw), cur),
            edge(prev), mid, edge(nxt),
            edge(prev), mid, edge(nxt),
            ctxb, ctxb,
        ],
        out_specs=pl.BlockSpec((1, qb * BLOCK, aw), cur),
        compiler_params=_cparams(2),
        name="window_attention",
    )(sink, q, k, k, k, v, v, v, kx, vx)


def _glob_kernel(bound_ref, q_ref, k_ref, v_ref, kx_ref, vx_ref, o_ref, m_ref, p_ref, *, kc):
    s = pl.program_id(0)
    ln = k_ref.shape[2]
    cn = kx_ref.shape[2]
    q = q_ref[0]
    tq = q.shape[0]
    q4 = _stack_heads(q, 0)
    pieces = [(k_ref, v_ref, c0, min(kc, ln - c0), c0) for c0 in range(0, ln, kc)] + [(kx_ref, vx_ref, 0, cn, ln)]

    @pl.when(s == 0)
    def _():
        p_ref[...] = jnp.zeros_like(p_ref)

    bound = bound_ref[0]
    m_ref[...] = jnp.full(m_ref.shape, bound - BOUND_SHIFT, F32)

    @pl.when(bound > BOUND_MAX)
    def _():
        m = None
        for kr, _, c0, n, _ in pieces:
            mc = _dot_nt(q4, kr[0, 0, c0:c0 + n, :]).max(axis=-1, keepdims=True)
            m = mc if m is None else jnp.maximum(m, mc)
        m_ref[...] = m

    m = m_ref[...]
    acc = None
    for kr, vr, c0, n, off in pieces:
        oc = _dot(p_ref[:, off:off + n], vr[0, 0, c0:c0 + n, :])
        acc = oc if acc is None else acc + oc
        p_ref[:, off:off + n] = jnp.exp2(_dot_nt(q4, kr[0, 0, c0:c0 + n, :]) - m).astype(BF16)
    l = acc[:, HEAD_DIM:HEAD_DIM + 1]
    l = jnp.where(l > 0.0, l, 1.0)
    o = acc[:, :HEAD_DIM] / l
    o_ref[0] = jnp.concatenate(_unstack_heads(o, tq), axis=1).astype(BF16)


def _score_bound(qg, kg):
    return (HEAD_DIM * Q_SCALE * BOUND_SLACK * jnp.max(jnp.abs(qg)) * jnp.max(jnp.abs(kg))).reshape(1)


def _global_attention(bound, q, k, v, kx, vx, *, tq, kc):
    bn, ln, aw = q.shape
    n_kv = k.shape[1]
    cn = kx.shape[2]
    gw = aw // n_kv
    assert gw == Q_PER_KV * HEAD_DIM
    nq = ln // tq
    n_tiles = bn * n_kv * nq

    def split(t):
        return t // (n_kv * nq), (t // nq) % n_kv, t % nq

    def qk_tile(s):
        return split(jnp.minimum(s, n_tiles - 1))

    def pv_tile(s):
        return split(jnp.maximum(s - 1, 0))

    def q_map(s):
        b, h, i = qk_tile(s)
        return b, i, h

    def k_map(s):
        b, h, _ = qk_tile(s)
        return b, h, 0, 0

    def v_map(s):
        b, h, _ = pv_tile(s)
        return b, h, 0, 0

    def o_map(s):
        b, h, i = pv_tile(s)
        return b, i, h

    return pl.pallas_call(
        functools.partial(_glob_kernel, kc=kc),
        out_shape=jax.ShapeDtypeStruct((bn, ln, aw), BF16),
        grid=(n_tiles + 1,),
        in_specs=[
            pl.BlockSpec(memory_space=pltpu.SMEM),
            pl.BlockSpec((1, tq, gw), q_map),
            pl.BlockSpec((1, 1, ln, HEAD_DIM), k_map),
            pl.BlockSpec((1, 1, ln, LANES), v_map),
            pl.BlockSpec((1, 1, cn, HEAD_DIM), k_map),
            pl.BlockSpec((1, 1, cn, LANES), v_map),
        ],
        out_specs=pl.BlockSpec((1, tq, gw), o_map),
        scratch_shapes=[pltpu.VMEM((Q_PER_KV * tq, 1), F32), pltpu.VMEM((Q_PER_KV * tq, ln + cn), BF16)],
        compiler_params=_cparams(1),
        name="global_attention",
    )(bound, q, k, v, kx, vx)


def _ctx_attn_kernel(sink_ref, wq_ref, wk_ref, wv_ref, gq_ref, gk_ref, gv_ref, ow_ref, og_ref):
    wq = wq_ref[0]
    gq = gq_ref[0]
    wk = wk_ref[0]
    cn, kw = wk.shape
    wv_aug = jnp.concatenate([wv_ref[0], _ones_column(cn)], axis=1)
    outs_w = []
    outs_g = []
    for kv in range(kw // HEAD_DIM):
        sl = slice(kv * HEAD_DIM, (kv + 1) * HEAD_DIM)
        s = _dot_nt(_stack_heads(wq, kv), wk[:, sl])
        sink = _sink_column(sink_ref, kv, cn)
        m = jnp.maximum(s.max(axis=-1, keepdims=True), sink)
        oa = _dot(jnp.exp2(s - m).astype(BF16), wv_aug)
        l = oa[:, kw:] + jnp.exp2(sink - m)
        outs_w += _unstack_heads((oa[:, :kw] / l)[:, sl], cn)
        s = _dot_nt(_stack_heads(gq, kv), gk_ref[0, kv])
        m = s.max(axis=-1, keepdims=True)
        oa = _dot(jnp.exp2(s - m).astype(BF16), gv_ref[0, kv])
        outs_g += _unstack_heads(oa[:, :HEAD_DIM] / oa[:, HEAD_DIM:HEAD_DIM + 1], cn)
    ow_ref[0] = jnp.concatenate(outs_w, axis=1).astype(BF16)
    og_ref[0] = jnp.concatenate(outs_g, axis=1).astype(BF16)


def _context_attention(sink, wq, wk, wv, gq, gk, gv):
    bn, cn, aw = wq.shape
    kw = wk.shape[2]
    n_kv = gk.shape[1]
    tok = lambda w: pl.BlockSpec((1, cn, w), lambda b: (b, 0, 0))
    hd = lambda w: pl.BlockSpec((1, n_kv, cn, w), lambda b: (b, 0, 0, 0))
    shp = jax.ShapeDtypeStruct((bn, cn, aw), BF16)
    return pl.pallas_call(
        _ctx_attn_kernel,
        out_shape=(shp, shp),
        grid=(bn,),
        in_specs=[pl.BlockSpec(memory_space=pltpu.SMEM), tok(aw), tok(kw), tok(kw), tok(aw),
                  hd(HEAD_DIM), hd(LANES)],
        out_specs=(tok(aw), tok(aw)),
        compiler_params=_cparams(1),
        name="context_attention",
    )(sink, wq, wk, wv, gq, gk, gv)


def _merge_kernel(x_ref, mod_ref, g1_ref, g2_ref, yp_ref, yw_ref, yg_ref, wg_ref, bg_ref, wb_ref, wo_ref,
                  x1_ref, h2_ref):
    x = x_ref[0]
    h = _rms_mod(x, g1_ref[...], mod_ref[0, 0:1, :], mod_ref[0, 1:2, :]).astype(BF16)
    merged = None
    for i, y_ref in enumerate((yp_ref, yw_ref, yg_ref)):
        gate = jax.nn.sigmoid(_dot(h, wg_ref[i]) + bg_ref[i:i + 1, :])
        term = gate * _dot(y_ref[0], wb_ref[i])
        merged = term if merged is None else merged + term
    x1 = x + mod_ref[0, 2:3, :] * _dot(merged.astype(BF16), wo_ref[...])
    x1_ref[0] = x1
    h2_ref[0] = _rms_mod(x1, g2_ref[...], mod_ref[0, 3:4, :], mod_ref[0, 4:5, :]).astype(BF16)


def _merge(x, mod, g1, g2, yp, yw, yg, wg, bg, wb, wo, *, tl):
    bn, ln, d = x.shape
    bw = yp.shape[2]
    mod_map = (lambda b, i: (b, 0, 0)) if mod.shape[0] > 1 else (lambda b, i: (0, 0, 0))
    tok = lambda w: pl.BlockSpec((1, tl, w), lambda b, i: (b, i, 0))
    full = _resident
    return pl.pallas_call(
        _merge_kernel,
        out_shape=(jax.ShapeDtypeStruct((bn, ln, d), F32), jax.ShapeDtypeStruct((bn, ln, d), BF16)),
        grid=(bn, ln // tl),
        in_specs=[tok(d), pl.BlockSpec((1, 6, d), mod_map), full(g1), full(g2), tok(bw), tok(bw), tok(bw),
                  full(wg), full(bg), full(wb), full(wo)],
        out_specs=(tok(d), tok(d)),
        compiler_params=_cparams(2),
        name="merge_branches",
    )(x, mod, g1, g2, yp, yw, yg, wg, bg, wb, wo)


def _ffn_kernel(h_ref, hp_ref, hn_ref, x_ref, mod_ref, wg_ref, wv_ref, cw_ref, cb_ref, wd_ref, fg_ref,
                o_ref, g_scr, *, final):
    j = pl.program_id(1)
    nj = pl.num_programs(1)
    tl = h_ref.shape[1]
    halo = hp_ref.shape[1]
    hm = h_ref[0]
    hp = jnp.where(j > 0, hp_ref[0], jnp.zeros_like(hp_ref[0]))
    hn = jnp.where(j < nj - 1, hn_ref[0], jnp.zeros_like(hn_ref[0]))
    hext = jnp.concatenate([hp, hm, hn], axis=0)
    g_scr[...] = _dot(hext, wg_ref[...])
    a = (g_scr[pl.ds(halo - 1, tl), :] * cw_ref[0:1, :]
         + g_scr[pl.ds(halo, tl), :] * cw_ref[1:2, :]
         + g_scr[pl.ds(halo + 1, tl), :] * cw_ref[2:3, :]
         + cb_ref[...])
    val = _dot(hm, wv_ref[...])
    act = (a * jax.nn.sigmoid(a)) * val
    x2 = x_ref[0] + mod_ref[0, 5:6, :] * _dot(act.astype(BF16), wd_ref[...])
    if final:
        x2 = x2 * lax.rsqrt(jnp.mean(x2 * x2, axis=-1, keepdims=True) + EPS) * fg_ref[...]
    o_ref[0] = x2


def _ffn(h2, x1, mod, wg, wv, cw, cb, wd, fg, *, tl, final):
    bn, ln, d = x1.shape
    f = wg.shape[1]
    halo = BF16_SUBLANES
    r = tl // halo
    nh = ln // halo
    mod_map = (lambda b, i: (b, 0, 0)) if mod.shape[0] > 1 else (lambda b, i: (0, 0, 0))
    tok = pl.BlockSpec((1, tl, d), lambda b, i: (b, i, 0))
    full = _resident
    return pl.pallas_call(
        functools.partial(_ffn_kernel, final=final),
        out_shape=jax.ShapeDtypeStruct((bn, ln, d), F32),
        grid=(bn, ln // tl),
        in_specs=[
            tok,
            pl.BlockSpec((1, halo, d), lambda b, i: (b, jnp.maximum(i * r - 1, 0), 0)),
            pl.BlockSpec((1, halo, d), lambda b, i: (b, jnp.minimum((i + 1) * r, nh - 1), 0)),
            tok,
            pl.BlockSpec((1, 6, d), mod_map),
            full(wg), full(wv), full(cw), full(cb), full(wd), full(fg),
        ],
        out_specs=tok,
        scratch_shapes=[pltpu.VMEM((tl + 2 * halo, f), F32)],
        compiler_params=_cparams(2),
        name="conv_glu_final" if final else "conv_glu",
    )(h2, h2, h2, x1, mod, wg, wv, cw, cb, wd, fg)


def _rope_tables(n_tok):
    pos = jnp.arange(n_tok)
    row = (pos // GRID_W).astype(F32)
    col = (pos % GRID_W).astype(F32)
    half = HEAD_DIM // 2
    inv = ROPE_THETA ** (-jnp.arange(0, half, 2, dtype=F32) / half)
    ang_r = row[:, None] * inv
    ang_c = col[:, None] * inv
    cos64 = jnp.concatenate([jnp.cos(ang_r)] * 2 + [jnp.cos(ang_c)] * 2, axis=1)
    sin64 = jnp.concatenate([-jnp.sin(ang_r), jnp.sin(ang_r), -jnp.sin(ang_c), jnp.sin(ang_c)], axis=1)
    reps = LANES // HEAD_DIM
    return jnp.tile(cos64, (1, reps)), jnp.tile(sin64, (1, reps))


def _tile_rows(n, target):
    t = min(n, target)
    while n % t:
        t //= 2
    return t


def kernel(x, c, ctx, c_ctx, w_mod, b_mod, norm1_g, norm2_g, w_in, w_pool_grp, pool_scale, win_sink,
           q_norm_g, k_norm_g, w_branch, w_gate, b_gate, w_out, w_ff_gate, w_ff_val, conv_w, conv_b,
           w_ff_down, final_g):
    bn, ln, d = x.shape
    cn = ctx.shape[1]
    depth = w_mod.shape[0]
    attn_w = d // 2
    kv_w = attn_w // Q_PER_KV
    n_heads = attn_w // HEAD_DIM

    mod_all = _modulation(jnp.concatenate([c, c_ctx[None]], axis=0), w_mod, b_mod)
    mod_all = mod_all.reshape(depth, bn + 1, 6, d)

    cos, sin = _rope_tables(ln)
    cos_c = jnp.ones((cn, LANES), F32)
    sin_c = jnp.zeros((cn, LANES), F32)
    ids = np.arange(2 * LANES) // HEAD_DIM
    bd = jnp.asarray(ids[:, None] == ids[None, :], dtype=BF16)
    fg = final_g.reshape(1, d)

    tl = _tile_rows(ln, 1024)
    tl_ffn = _tile_rows(ln, 512)
    tl_c = _tile_rows(cn, 256)
    tq = _tile_rows(ln, 256)
    win_qb = _tile_rows(ln // BLOCK, 4)

    for l in range(depth):
        last = l == depth - 1
        mod_l = mod_all[l, :bn]
        mod_c = mod_all[l, bn:]
        g1 = norm1_g[l].reshape(1, d)
        g2 = norm2_g[l].reshape(1, d)
        w_in_l = w_in[l].astype(BF16)
        qg = jnp.tile(q_norm_g[l], attn_w // HEAD_DIM).reshape(1, attn_w)
        kg = jnp.tile(k_norm_g[l], kv_w // HEAD_DIM).reshape(1, kv_w)
        w_grp = w_pool_grp[l].astype(BF16)
        p_scale = pool_scale[l].reshape(1, -1)
        sink = win_sink[l]
        wg = w_gate[l].astype(BF16)
        bg = b_gate[l]
        wb = w_branch[l].astype(BF16)
        wo = w_out[l].astype(BF16)
        wfg = w_ff_gate[l].astype(BF16)
        wfv = w_ff_val[l].astype(BF16)
        wfd = w_ff_down[l].astype(BF16)
        cw = conv_w[l]
        cb = conv_b[l].reshape(1, -1)

        u_l, wq_l, wk_l, wv_l, gq_l, gk_l, gv_l = _inproj(
            x, mod_l, g1, w_in_l, cos, sin, qg, kg, bd, rope=True, tl=tl)
        u_c, wq_c, wk_c, wv_c, gq_c, gk_c, gv_c = _inproj(
            ctx, mod_c, g1, w_in_l, cos_c, sin_c, qg, kg, bd, rope=False, tl=tl_c)

        y_pool = _pool(u_l, w_grp, p_scale)
        y_win = _window_attention(sink, wq_l, wk_l, wv_l, wk_c, wv_c, qb=win_qb)
        bound = _score_bound(q_norm_g[l], k_norm_g[l])
        y_glob = _global_attention(bound, gq_l, gk_l, gv_l, gk_c, gv_c, tq=tq, kc=512)
        x1, h2 = _merge(x, mod_l, g1, g2, y_pool, y_win, y_glob, wg, bg, wb, wo, tl=tl)

        if not last:
            yc_pool = _pool(u_c, w_grp, p_scale)
            yc_win, yc_glob = _context_attention(sink, wq_c, wk_c, wv_c, gq_c, gk_c, gv_c)
            c1, hc2 = _merge(ctx, mod_c, g1, g2, yc_pool, yc_win, yc_glob, wg, bg, wb, wo, tl=tl_c)
            ctx = _ffn(hc2, c1, mod_c, wfg, wfv, cw, cb, wfd, fg, tl=tl_c, final=False)

        x = _ffn(h2, x1, mod_l, wfg, wfv, cw, cb, wfd, fg, tl=tl_ffn, final=last)
    return x
```

```python
import functools

import jax
import jax.numpy as jnp
import numpy as np
from jax import lax
from jax.experimental import pallas as pl
from jax.experimental.pallas import tpu as pltpu

F32 = jnp.float32
BF16 = jnp.bfloat16

GRID_W = 64
HEAD_DIM = 64
Q_PER_KV = 4
POOL_WINDOWS = (2, 4, 8, 16)
POOL_HALO = 16
WINDOW = 128
BLOCK = 128
ROPE_THETA = 10000.0
EPS = 1e-6
NEG = -1e30
SM_SCALE = HEAD_DIM ** -0.5
LOG2E = 1.4426950408889634
Q_SCALE = SM_SCALE * LOG2E
BOUND_SLACK = 1.02
BOUND_SHIFT = 60.0
BOUND_MAX = 80.0
LANES = 128
BF16_SUBLANES = 16
VMEM_LIMIT = 56 * 1024 * 1024


def _cparams(n_axes):
    return pltpu.CompilerParams(
        dimension_semantics=("arbitrary",) * n_axes,
        vmem_limit_bytes=VMEM_LIMIT,
    )


def _resident(a):
    return pl.BlockSpec(a.shape, lambda *_: (0,) * a.ndim, pipeline_mode=pl.Buffered(1))


def _rms_mod(x, g, shift, scale):
    y = x * lax.rsqrt(jnp.mean(x * x, axis=-1, keepdims=True) + EPS)
    return (y * g) * (1.0 + scale) + shift


def _dot(a, b):
    return jnp.dot(a, b, preferred_element_type=F32)


def _dot_nt(a, b):
    return lax.dot_general(a, b, (((1,), (1,)), ((), ())), preferred_element_type=F32)


def _mod_kernel(c_ref, w_ref, b_ref, o_ref):
    c = c_ref[...]
    s = c * jax.nn.sigmoid(c)
    o_ref[0] = _dot(s.astype(BF16), w_ref[0].astype(BF16)) + b_ref[0]


def _modulation(cc, w_mod, b_mod):
    depth, d, n = w_mod.shape
    rows = cc.shape[0]
    tn = 1536
    return pl.pallas_call(
        _mod_kernel,
        out_shape=jax.ShapeDtypeStruct((depth, rows, n), F32),
        grid=(depth, n // tn),
        in_specs=[
            pl.BlockSpec((rows, d), lambda l, j: (0, 0)),
            pl.BlockSpec((1, d, tn), lambda l, j: (l, 0, j)),
            pl.BlockSpec((1, 1, tn), lambda l, j: (l, 0, j)),
        ],
        out_specs=pl.BlockSpec((1, rows, tn), lambda l, j: (l, 0, j)),
        compiler_params=_cparams(2),
        name="modulation",
    )(cc, w_mod, b_mod.reshape(depth, 1, n))


def _head_rms(t, bd, g):
    sq = t * t
    hi = sq.astype(BF16)
    lo = (sq - hi.astype(F32)).astype(BF16)
    w = t.shape[1]
    parts = []
    for j in range(0, w, 2 * LANES):
        e = min(j + 2 * LANES, w)
        b = bd[: e - j, : e - j]
        parts.append(_dot(hi[:, j:e], b) + _dot(lo[:, j:e], b))
    ssq = parts[0] if len(parts) == 1 else jnp.concatenate(parts, axis=1)
    return t * lax.rsqrt(ssq * (1.0 / HEAD_DIM) + EPS) * g


def _rope(t, cos, sin, partner_lane):
    outs = []
    for j in range(0, t.shape[1], LANES):
        tj = t[:, j:j + LANES]
        partner = jnp.take_along_axis(tj, partner_lane, axis=1)
        outs.append(tj * cos + partner * sin)
    return outs[0] if len(outs) == 1 else jnp.concatenate(outs, axis=1)


def _inproj_kernel(x_ref, mod_ref, g1_ref, w_ref, cos_ref, sin_ref, qg_ref, kg_ref, bd_ref,
                   u_ref, wq_ref, wk_ref, wv_ref, gq_ref, gk_ref, gv_ref, *, rope, pool_w, attn_w, kv_w):
    x = x_ref[0]
    h = _rms_mod(x, g1_ref[...], mod_ref[0, 0:1, :], mod_ref[0, 1:2, :])
    h = h.astype(BF16)
    mix_w = attn_w + 2 * kv_w
    zg = _dot(h, w_ref[:, pool_w + mix_w:pool_w + 2 * mix_w])
    zw = _dot(h, w_ref[:, pool_w:pool_w + mix_w])
    u_ref[0] = _dot(h, w_ref[:, :pool_w])
    gq, gk, gv = zg[:, :attn_w], zg[:, attn_w:attn_w + kv_w], zg[:, attn_w + kv_w:]
    wq, wk, wv = zw[:, :attn_w], zw[:, attn_w:attn_w + kv_w], zw[:, attn_w + kv_w:]
    bd = bd_ref[...]
    gq = _head_rms(gq, bd, qg_ref[...])
    gk = _head_rms(gk, bd, kg_ref[...])
    if rope:
        cos = cos_ref[...]
        sin = sin_ref[...]
        lane = lax.broadcasted_iota(jnp.int32, cos.shape, 1)
        partner_lane = lane ^ 16
        wq = _rope(wq, cos, sin, partner_lane)
        wk = _rope(wk, cos, sin, partner_lane)
        gq = _rope(gq, cos, sin, partner_lane)
        gk = _rope(gk, cos, sin, partner_lane)
    wq_ref[0] = (wq * Q_SCALE).astype(BF16)
    gq_ref[0] = (gq * Q_SCALE).astype(BF16)
    wv_ref[0] = wv.astype(BF16)
    wk_ref[0] = wk.astype(BF16)
    lane_v = lax.broadcasted_iota(jnp.int32, gv.shape, 1)
    ones_col = jnp.where(lane_v == HEAD_DIM, 1.0, 0.0)
    for k in range(kv_w // HEAD_DIM):
        gk_ref[0, k] = gk[:, k * HEAD_DIM:(k + 1) * HEAD_DIM].astype(BF16)
        gvk = gv if k == 0 else pltpu.roll(gv, kv_w - k * HEAD_DIM, axis=1)
        gv_ref[0, k] = jnp.where(lane_v < HEAD_DIM, gvk, ones_col).astype(BF16)


def _inproj(x, mod, g1, w_in, cos, sin, qg, kg, bd, *, rope, tl):
    bn, ln, d = x.shape
    in_w = w_in.shape[1]
    attn_w = qg.shape[1]
    kv_w = kg.shape[1]
    pool_w = in_w - 2 * attn_w - 4 * kv_w
    n_kv = kv_w // HEAD_DIM
    mod_b = mod.shape[0]
    mod_map = (lambda b, i: (b, 0, 0)) if mod_b > 1 else (lambda b, i: (0, 0, 0))
    tok = lambda w: pl.BlockSpec((1, tl, w), lambda b, i: (b, i, 0))
    full2 = _resident
    kern = functools.partial(_inproj_kernel, rope=rope, pool_w=pool_w, attn_w=attn_w, kv_w=kv_w)
    out_shapes = (
        jax.ShapeDtypeStruct((bn, ln, pool_w), F32),
        jax.ShapeDtypeStruct((bn, ln, attn_w), BF16),
        jax.ShapeDtypeStruct((bn, ln, kv_w), BF16),
        jax.ShapeDtypeStruct((bn, ln, kv_w), BF16),
        jax.ShapeDtypeStruct((bn, ln, attn_w), BF16),
        jax.ShapeDtypeStruct((bn, n_kv, ln, HEAD_DIM), BF16),
        jax.ShapeDtypeStruct((bn, n_kv, ln, LANES), BF16),
    )
    assert kv_w == LANES
    head_spec = lambda w: pl.BlockSpec((1, n_kv, tl, w), lambda b, i: (b, 0, i, 0))
    return pl.pallas_call(
        kern,
        out_shape=out_shapes,
        grid=(bn, ln // tl),
        in_specs=[
            tok(d),
            pl.BlockSpec((1, 6, d), mod_map),
            full2(g1),
            full2(w_in),
            pl.BlockSpec((tl, LANES), lambda b, i: (i, 0)),
            pl.BlockSpec((tl, LANES), lambda b, i: (i, 0)),
            full2(qg),
            full2(kg),
            full2(bd),
        ],
        out_specs=(tok(pool_w), tok(attn_w), tok(kv_w), tok(kv_w), tok(attn_w), head_spec(HEAD_DIM), head_spec(LANES)),
        compiler_params=_cparams(2),
        name="inproj_rope" if rope else "inproj_ctx",
    )(x, mod, g1, w_in, cos, sin, qg, kg, bd)


def _pool_kernel(u_ref, w_ref, s_ref, o_ref, ext_ref, *, ln):
    n_grp = len(POOL_WINDOWS)
    gw = u_ref.shape[2] // n_grp
    t = lax.broadcasted_iota(jnp.int32, (ln, 1), 0)
    ext_len = ln + 2 * POOL_HALO
    zeros = jnp.zeros((POOL_HALO, gw), F32)
    for gi, win in enumerate(POOL_WINDOWS):
        u = u_ref[0, :, gi * gw:(gi + 1) * gw]
        ext_ref[0:POOL_HALO, :] = zeros
        ext_ref[POOL_HALO + ln:ext_len, :] = zeros
        ext_ref[POOL_HALO:POOL_HALO + ln, :] = u
        e = ext_ref[...]
        acc = e + pltpu.roll(e, 1, axis=0)
        half = 1
        while 2 * half < win:
            acc = pltpu.roll(acc, half, axis=0) + pltpu.roll(acc, ext_len - half, axis=0)
            half *= 2
        wsum = acc[POOL_HALO:POOL_HALO + ln, :]
        cnt = (jnp.minimum(t + win // 2, ln) - jnp.maximum(t - win // 2, 0)).astype(F32)
        p = wsum / cnt - u
        y = _dot(p.astype(BF16), w_ref[gi]) * s_ref[:, gi * gw:(gi + 1) * gw]
        o_ref[0, :, gi * gw:(gi + 1) * gw] = y.astype(BF16)


def _pool(u, w_grp, scale):
    bn, ln, pw = u.shape
    gw = pw // len(POOL_WINDOWS)
    return pl.pallas_call(
        functools.partial(_pool_kernel, ln=ln),
        out_shape=jax.ShapeDtypeStruct((bn, ln, pw), BF16),
        grid=(bn,),
        in_specs=[
            pl.BlockSpec((1, ln, pw), lambda b: (b, 0, 0)),
            pl.BlockSpec(w_grp.shape, lambda b: (0, 0, 0)),
            pl.BlockSpec(scale.shape, lambda b: (0, 0)),
        ],
        out_specs=pl.BlockSpec((1, ln, pw), lambda b: (b, 0, 0)),
        scratch_shapes=[pltpu.VMEM((ln + 2 * POOL_HALO, gw), F32)],
        compiler_params=_cparams(1),
        name="pool_mixer",
    )(u, w_grp, scale)


def _stack_heads(q, kv):
    return jnp.concatenate(
        [q[:, (kv * Q_PER_KV + g) * HEAD_DIM:(kv * Q_PER_KV + g + 1) * HEAD_DIM] for g in range(Q_PER_KV)], axis=0)


def _unstack_heads(o, t):
    return [o[g * t:(g + 1) * t] for g in range(Q_PER_KV)]


def _sink_column(sink_ref, kv, t):
    row = lax.broadcasted_iota(jnp.int32, (Q_PER_KV * t, 1), 0)
    col = jnp.full((Q_PER_KV * t, 1), sink_ref[kv * Q_PER_KV + Q_PER_KV - 1] * LOG2E, F32)
    for g in range(Q_PER_KV - 2, -1, -1):
        col = jnp.where(row < (g + 1) * t, sink_ref[kv * Q_PER_KV + g] * LOG2E, col)
    return col


def _ones_column(n):
    return jnp.ones((n, LANES), BF16)


def _win_kernel(sink_ref, q_ref, kp_ref, kc_ref, kn_ref, vp_ref, vc_ref, vn_ref, kx_ref, vx_ref, o_ref,
                pl_ref, px_ref, es_ref, *, qb, ni):
    t = pl.program_id(0)
    i = jnp.minimum(t, pl.num_programs(0) - 2) % ni
    kall = jnp.concatenate([kp_ref[0], kc_ref[0], kn_ref[0]], axis=0)
    vall = jnp.concatenate([vp_ref[0], vc_ref[0], vn_ref[0]], axis=0)
    kx = kx_ref[0]
    vall_aug = jnp.concatenate([vall, _ones_column(vall.shape[0])], axis=1)
    vx_aug = jnp.concatenate([vx_ref[0], _ones_column(kx.shape[0])], axis=1)
    kw = kall.shape[1]
    n_kv = kw // HEAD_DIM
    row = lax.broadcasted_iota(jnp.int32, (BLOCK, 3 * BLOCK), 0)
    col = lax.broadcasted_iota(jnp.int32, (BLOCK, 3 * BLOCK), 1)
    band = jnp.where(jnp.abs(col - BLOCK - row) <= WINDOW, 0.0, NEG)
    no_prev = jnp.where((col < BLOCK) & (i == 0), NEG, 0.0)
    no_next = jnp.where((col >= 2 * BLOCK) & (i == ni - 1), NEG, 0.0)

    @pl.when(t == 0)
    def _():
        pl_ref[...] = jnp.zeros_like(pl_ref)
        px_ref[...] = jnp.zeros_like(px_ref)
        es_ref[...] = jnp.ones_like(es_ref)

    blocks = []
    for j in range(qb):
        bias = band
        if j == 0:
            bias = bias + no_prev
        if j == qb - 1:
            bias = bias + no_next
        bias = jnp.concatenate([bias] * Q_PER_KV, axis=0)
        q = q_ref[0, j * BLOCK:(j + 1) * BLOCK, :]
        kwin = kall[j * BLOCK:(j + 3) * BLOCK]
        vwin_aug = vall_aug[j * BLOCK:(j + 3) * BLOCK]
        outs = []
        for kv in range(n_kv):
            g = j * n_kv + kv
            sl = slice(kv * HEAD_DIM, (kv + 1) * HEAD_DIM)
            oa = _dot(pl_ref[g], vwin_aug) + _dot(px_ref[g], vx_aug)
            l = oa[:, kw:] + es_ref[g]
            outs += _unstack_heads((oa[:, :kw] / l)[:, sl], BLOCK)
            q4 = _stack_heads(q, kv)
            s_lat = _dot_nt(q4, kwin[:, sl]) + bias
            s_ctx = _dot_nt(q4, kx[:, sl])
            sink = _sink_column(sink_ref, kv, BLOCK)
            m = jnp.maximum(jnp.concatenate([s_lat, s_ctx], axis=1).max(axis=-1, keepdims=True), sink)
            pl_ref[g] = jnp.exp2(s_lat - m).astype(BF16)
            px_ref[g] = jnp.exp2(s_ctx - m).astype(BF16)
            es_ref[g] = jnp.broadcast_to(jnp.exp2(sink - m), es_ref.shape[1:])
        blocks.append(jnp.concatenate(outs, axis=1).astype(BF16))
    o_ref[0] = blocks[0] if qb == 1 else jnp.concatenate(blocks, axis=0)


def _window_attention(sink, q, k, v, kx, vx, *, qb):
    bn, ln, aw = q.shape
    kw = k.shape[2]
    cn = kx.shape[1]
    nb = ln // BLOCK
    assert kw == LANES and nb % qb == 0
    ni = nb // qb
    n_tiles = bn * ni
    n_grp = qb * (kw // HEAD_DIM)

    def qk_tile(t):
        t = jnp.minimum(t, n_tiles - 1)
        return t // ni, t % ni

    def pv_tile(t):
        t = jnp.maximum(t - 1, 0)
        return t // ni, t % ni

    def maps(tile):
        cur = lambda t: (tile(t)[0], tile(t)[1], 0)
        prev = lambda t: (tile(t)[0], jnp.maximum(tile(t)[1] * qb - 1, 0), 0)
        nxt = lambda t: (tile(t)[0], jnp.minimum((tile(t)[1] + 1) * qb, nb - 1), 0)
        ctx = lambda t: (tile(t)[0], 0, 0)
        return cur, prev, nxt, ctx

    edge = lambda f: pl.BlockSpec((1, BLOCK, kw), f)
    mid = lambda f: pl.BlockSpec((1, qb * BLOCK, kw), f)
    ctxb = lambda f: pl.BlockSpec((1, cn, kw), f)
    k_cur, k_prev, k_nxt, k_ctx = maps(qk_tile)
    v_cur, v_prev, v_nxt, v_ctx = maps(pv_tile)
    return pl.pallas_call(
        functools.partial(_win_kernel, qb=qb, ni=ni),
        out_shape=jax.ShapeDtypeStruct((bn, ln, aw), BF16),
        grid=(n_tiles + 1,),
        in_specs=[
            pl.BlockSpec(memory_space=pltpu.SMEM),
            pl.BlockSpec((1, qb * BLOCK, aw), k_cur),
            edge(k_prev), mid(k_cur), edge(k_nxt),
            edge(v_prev), mid(v_cur), edge(v_nxt),
            ctxb(k_ctx), ctxb(v_ctx),
        ],
        out_specs=pl.BlockSpec((1, qb * BLOCK, aw), v_cur),
        scratch_shapes=[
            pltpu.VMEM((n_grp, Q_PER_KV * BLOCK, 3 * BLOCK), BF16),
            pltpu.VMEM((n_grp, Q_PER_KV * BLOCK, cn), BF16),
            pltpu.VMEM((n_grp, Q_PER_KV * BLOCK, LANES), F32),
        ],
        compiler_params=_cparams(1),
        name="window_attention",
    )(sink, q, k, k, k, v, v, v, kx, vx)


def _glob_kernel(bound_ref, q_ref, k_ref, v_ref, kx_ref, vx_ref, o_ref, m_ref, p_ref, *, kc):
    s = pl.program_id(0)
    ln = k_ref.shape[2]
    cn = kx_ref.shape[2]
    q = q_ref[0]
    tq = q.shape[0]
    q4 = _stack_heads(q, 0)
    pieces = [(k_ref, v_ref, c0, min(kc, ln - c0), c0) for c0 in range(0, ln, kc)] + [(kx_ref, vx_ref, 0, cn, ln)]

    @pl.when(s == 0)
    def _():
        p_ref[...] = jnp.zeros_like(p_ref)

    bound = bound_ref[0]
    m_ref[...] = jnp.full(m_ref.shape, bound - BOUND_SHIFT, F32)

    @pl.when(bound > BOUND_MAX)
    def _():
        m = None
        for kr, _, c0, n, _ in pieces:
            mc = _dot_nt(q4, kr[0, 0, c0:c0 + n, :]).max(axis=-1, keepdims=True)
            m = mc if m is None else jnp.maximum(m, mc)
        m_ref[...] = m

    m = m_ref[...]
    acc = None
    for kr, vr, c0, n, off in pieces:
        oc = _dot(p_ref[:, off:off + n], vr[0, 0, c0:c0 + n, :])
        acc = oc if acc is None else acc + oc
        p_ref[:, off:off + n] = jnp.exp2(_dot_nt(q4, kr[0, 0, c0:c0 + n, :]) - m).astype(BF16)
    l = acc[:, HEAD_DIM:HEAD_DIM + 1]
    l = jnp.where(l > 0.0, l, 1.0)
    o = acc[:, :HEAD_DIM] / l
    o_ref[0] = jnp.concatenate(_unstack_heads(o, tq), axis=1).astype(BF16)


def _score_bound(qg, kg):
    return (HEAD_DIM * Q_SCALE * BOUND_SLACK * jnp.max(jnp.abs(qg)) * jnp.max(jnp.abs(kg))).reshape(1)


def _global_attention(bound, q, k, v, kx, vx, *, tq, kc):
    bn, ln, aw = q.shape
    n_kv = k.shape[1]
    cn = kx.shape[2]
    gw = aw // n_kv
    assert gw == Q_PER_KV * HEAD_DIM
    nq = ln // tq
    n_tiles = bn * n_kv * nq

    def split(t):
        return t // (n_kv * nq), (t // nq) % n_kv, t % nq

    def qk_tile(s):
        return split(jnp.minimum(s, n_tiles - 1))

    def pv_tile(s):
        return split(jnp.maximum(s - 1, 0))

    def q_map(s):
        b, h, i = qk_tile(s)
        return b, i, h

    def k_map(s):
        b, h, _ = qk_tile(s)
        return b, h, 0, 0

    def v_map(s):
        b, h, _ = pv_tile(s)
        return b, h, 0, 0

    def o_map(s):
        b, h, i = pv_tile(s)
        return b, i, h

    return pl.pallas_call(
        functools.partial(_glob_kernel, kc=kc),
        out_shape=jax.ShapeDtypeStruct((bn, ln, aw), BF16),
        grid=(n_tiles + 1,),
        in_specs=[
            pl.BlockSpec(memory_space=pltpu.SMEM),
            pl.BlockSpec((1, tq, gw), q_map),
            pl.BlockSpec((1, 1, ln, HEAD_DIM), k_map),
            pl.BlockSpec((1, 1, ln, LANES), v_map),
            pl.BlockSpec((1, 1, cn, HEAD_DIM), k_map),
            pl.BlockSpec((1, 1, cn, LANES), v_map),
        ],
        out_specs=pl.BlockSpec((1, tq, gw), o_map),
        scratch_shapes=[pltpu.VMEM((Q_PER_KV * tq, 1), F32), pltpu.VMEM((Q_PER_KV * tq, ln + cn), BF16)],
        compiler_params=_cparams(1),
        name="global_attention",
    )(bound, q, k, v, kx, vx)


def _ctx_attn_kernel(sink_ref, wq_ref, wk_ref, wv_ref, gq_ref, gk_ref, gv_ref, ow_ref, og_ref):
    wq = wq_ref[0]
    gq = gq_ref[0]
    wk = wk_ref[0]
    cn, kw = wk.shape
    wv_aug = jnp.concatenate([wv_ref[0], _ones_column(cn)], axis=1)
    outs_w = []
    outs_g = []
    for kv in range(kw // HEAD_DIM):
        sl = slice(kv * HEAD_DIM, (kv + 1) * HEAD_DIM)
        s = _dot_nt(_stack_heads(wq, kv), wk[:, sl])
        sink = _sink_column(sink_ref, kv, cn)
        m = jnp.maximum(s.max(axis=-1, keepdims=True), sink)
        oa = _dot(jnp.exp2(s - m).astype(BF16), wv_aug)
        l = oa[:, kw:] + jnp.exp2(sink - m)
        outs_w += _unstack_heads((oa[:, :kw] / l)[:, sl], cn)
        s = _dot_nt(_stack_heads(gq, kv), gk_ref[0, kv])
        m = s.max(axis=-1, keepdims=True)
        oa = _dot(jnp.exp2(s - m).astype(BF16), gv_ref[0, kv])
        outs_g += _unstack_heads(oa[:, :HEAD_DIM] / oa[:, HEAD_DIM:HEAD_DIM + 1], cn)
    ow_ref[0] = jnp.concatenate(outs_w, axis=1).astype(BF16)
    og_ref[0] = jnp.concatenate(outs_g, axis=1).astype(BF16)


def _context_attention(sink, wq, wk, wv, gq, gk, gv):
    bn, cn, aw = wq.shape
    kw = wk.shape[2]
    n_kv = gk.shape[1]
    tok = lambda w: pl.BlockSpec((1, cn, w), lambda b: (b, 0, 0))
    hd = lambda w: pl.BlockSpec((1, n_kv, cn, w), lambda b: (b, 0, 0, 0))
    shp = jax.ShapeDtypeStruct((bn, cn, aw), BF16)
    return pl.pallas_call(
        _ctx_attn_kernel,
        out_shape=(shp, shp),
        grid=(bn,),
        in_specs=[pl.BlockSpec(memory_space=pltpu.SMEM), tok(aw), tok(kw), tok(kw), tok(aw),
                  hd(HEAD_DIM), hd(LANES)],
        out_specs=(tok(aw), tok(aw)),
        compiler_params=_cparams(1),
        name="context_attention",
    )(sink, wq, wk, wv, gq, gk, gv)


def _merge_kernel(x_ref, mod_ref, g1_ref, g2_ref, yp_ref, yw_ref, yg_ref, wg_ref, bg_ref, wb_ref, wo_ref,
                  x1_ref, h2_ref):
    x = x_ref[0]
    h = _rms_mod(x, g1_ref[...], mod_ref[0, 0:1, :], mod_ref[0, 1:2, :]).astype(BF16)
    merged = None
    for i, y_ref in enumerate((yp_ref, yw_ref, yg_ref)):
        gate = jax.nn.sigmoid(_dot(h, wg_ref[i]) + bg_ref[i:i + 1, :])
        term = gate * _dot(y_ref[0], wb_ref[i])
        merged = term if merged is None else merged + term
    x1 = x + mod_ref[0, 2:3, :] * _dot(merged.astype(BF16), wo_ref[...])
    x1_ref[0] = x1
    h2_ref[0] = _rms_mod(x1, g2_ref[...], mod_ref[0, 3:4, :], mod_ref[0, 4:5, :]).astype(BF16)


def _merge(x, mod, g1, g2, yp, yw, yg, wg, bg, wb, wo, *, tl):
    bn, ln, d = x.shape
    bw = yp.shape[2]
    mod_map = (lambda b, i: (b, 0, 0)) if mod.shape[0] > 1 else (lambda b, i: (0, 0, 0))
    tok = lambda w: pl.BlockSpec((1, tl, w), lambda b, i: (b, i, 0))
    full = _resident
    return pl.pallas_call(
        _merge_kernel,
        out_shape=(jax.ShapeDtypeStruct((bn, ln, d), F32), jax.ShapeDtypeStruct((bn, ln, d), BF16)),
        grid=(bn, ln // tl),
        in_specs=[tok(d), pl.BlockSpec((1, 6, d), mod_map), full(g1), full(g2), tok(bw), tok(bw), tok(bw),
                  full(wg), full(bg), full(wb), full(wo)],
        out_specs=(tok(d), tok(d)),
        compiler_params=_cparams(2),
        name="merge_branches",
    )(x, mod, g1, g2, yp, yw, yg, wg, bg, wb, wo)


def _ffn_kernel(h_ref, hp_ref, hn_ref, x_ref, mod_ref, wg_ref, wv_ref, cw_ref, cb_ref, wd_ref, fg_ref,
                o_ref, g_scr, *, final):
    j = pl.program_id(1)
    nj = pl.num_programs(1)
    tl = h_ref.shape[1]
    halo = hp_ref.shape[1]
    hm = h_ref[0]
    hp = jnp.where(j > 0, hp_ref[0], jnp.zeros_like(hp_ref[0]))
    hn = jnp.where(j < nj - 1, hn_ref[0], jnp.zeros_like(hn_ref[0]))
    hext = jnp.concatenate([hp, hm, hn], axis=0)
    g_scr[...] = _dot(hext, wg_ref[...])
    a = (g_scr[pl.ds(halo - 1, tl), :] * cw_ref[0:1, :]
         + g_scr[pl.ds(halo, tl), :] * cw_ref[1:2, :]
         + g_scr[pl.ds(halo + 1, tl), :] * cw_ref[2:3, :]
         + cb_ref[...])
    val = _dot(hm, wv_ref[...])
    act = (a * jax.nn.sigmoid(a)) * val
    x2 = x_ref[0] + mod_ref[0, 5:6, :] * _dot(act.astype(BF16), wd_ref[...])
    if final:
        x2 = x2 * lax.rsqrt(jnp.mean(x2 * x2, axis=-1, keepdims=True) + EPS) * fg_ref[...]
    o_ref[0] = x2


def _ffn(h2, x1, mod, wg, wv, cw, cb, wd, fg, *, tl, final):
    bn, ln, d = x1.shape
    f = wg.shape[1]
    halo = BF16_SUBLANES
    r = tl // halo
    nh = ln // halo
    mod_map = (lambda b, i: (b, 0, 0)) if mod.shape[0] > 1 else (lambda b, i: (0, 0, 0))
    tok = pl.BlockSpec((1, tl, d), lambda b, i: (b, i, 0))
    full = _resident
    return pl.pallas_call(
        functools.partial(_ffn_kernel, final=final),
        out_shape=jax.ShapeDtypeStruct((bn, ln, d), F32),
        grid=(bn, ln // tl),
        in_specs=[
            tok,
            pl.BlockSpec((1, halo, d), lambda b, i: (b, jnp.maximum(i * r - 1, 0), 0)),
            pl.BlockSpec((1, halo, d), lambda b, i: (b, jnp.minimum((i + 1) * r, nh - 1), 0)),
            tok,
            pl.BlockSpec((1, 6, d), mod_map),
            full(wg), full(wv), full(cw), full(cb), full(wd), full(fg),
        ],
        out_specs=tok,
        scratch_shapes=[pltpu.VMEM((tl + 2 * halo, f), F32)],
        compiler_params=_cparams(2),
        name="conv_glu_final" if final else "conv_glu",
    )(h2, h2, h2, x1, mod, wg, wv, cw, cb, wd, fg)


def _rope_tables(n_tok):
    pos = jnp.arange(n_tok)
    row = (pos // GRID_W).astype(F32)
    col = (pos % GRID_W).astype(F32)
    half = HEAD_DIM // 2
    inv = ROPE_THETA ** (-jnp.arange(0, half, 2, dtype=F32) / half)
    ang_r = row[:, None] * inv
    ang_c = col[:, None] * inv
    cos64 = jnp.concatenate([jnp.cos(ang_r)] * 2 + [jnp.cos(ang_c)] * 2, axis=1)
    sin64 = jnp.concatenate([-jnp.sin(ang_r), jnp.sin(ang_r), -jnp.sin(ang_c), jnp.sin(ang_c)], axis=1)
    reps = LANES // HEAD_DIM
    return jnp.tile(cos64, (1, reps)), jnp.tile(sin64, (1, reps))


def _tile_rows(n, target):
    t = min(n, target)
    while n % t:
        t //= 2
    return t


def kernel(x, c, ctx, c_ctx, w_mod, b_mod, norm1_g, norm2_g, w_in, w_pool_grp, pool_scale, win_sink,
           q_norm_g, k_norm_g, w_branch, w_gate, b_gate, w_out, w_ff_gate, w_ff_val, conv_w, conv_b,
           w_ff_down, final_g):
    bn, ln, d = x.shape
    cn = ctx.shape[1]
    depth = w_mod.shape[0]
    attn_w = d // 2
    kv_w = attn_w // Q_PER_KV
    n_heads = attn_w // HEAD_DIM

    mod_all = _modulation(jnp.concatenate([c, c_ctx[None]], axis=0), w_mod, b_mod)
    mod_all = mod_all.reshape(depth, bn + 1, 6, d)

    cos, sin = _rope_tables(ln)
    cos_c = jnp.ones((cn, LANES), F32)
    sin_c = jnp.zeros((cn, LANES), F32)
    ids = np.arange(2 * LANES) // HEAD_DIM
    bd = jnp.asarray(ids[:, None] == ids[None, :], dtype=BF16)
    fg = final_g.reshape(1, d)

    tl = _tile_rows(ln, 1024)
    tl_ffn = _tile_rows(ln, 512)
    tl_c = _tile_rows(cn, 256)
    tq = _tile_rows(ln, 512)
    win_qb = _tile_rows(ln // BLOCK, 4)

    for l in range(depth):
        last = l == depth - 1
        mod_l = mod_all[l, :bn]
        mod_c = mod_all[l, bn:]
        g1 = norm1_g[l].reshape(1, d)
        g2 = norm2_g[l].reshape(1, d)
        w_in_l = w_in[l].astype(BF16)
        qg = jnp.tile(q_norm_g[l], attn_w // HEAD_DIM).reshape(1, attn_w)
        kg = jnp.tile(k_norm_g[l], kv_w // HEAD_DIM).reshape(1, kv_w)
        w_grp = w_pool_grp[l].astype(BF16)
        p_scale = pool_scale[l].reshape(1, -1)
        sink = win_sink[l]
        wg = w_gate[l].astype(BF16)
        bg = b_gate[l]
        wb = w_branch[l].astype(BF16)
        wo = w_out[l].astype(BF16)
        wfg = w_ff_gate[l].astype(BF16)
        wfv = w_ff_val[l].astype(BF16)
        wfd = w_ff_down[l].astype(BF16)
        cw = conv_w[l]
        cb = conv_b[l].reshape(1, -1)

        u_l, wq_l, wk_l, wv_l, gq_l, gk_l, gv_l = _inproj(
            x, mod_l, g1, w_in_l, cos, sin, qg, kg, bd, rope=True, tl=tl)
        u_c, wq_c, wk_c, wv_c, gq_c, gk_c, gv_c = _inproj(
            ctx, mod_c, g1, w_in_l, cos_c, sin_c, qg, kg, bd, rope=False, tl=tl_c)

        y_pool = _pool(u_l, w_grp, p_scale)
        y_win = _window_attention(sink, wq_l, wk_l, wv_l, wk_c, wv_c, qb=win_qb)
        bound = _score_bound(q_norm_g[l], k_norm_g[l])
        y_glob = _global_attention(bound, gq_l, gk_l, gv_l, gk_c, gv_c, tq=tq, kc=512)
        x1, h2 = _merge(x, mod_l, g1, g2, y_pool, y_win, y_glob, wg, bg, wb, wo, tl=tl)

        if not last:
            yc_pool = _pool(u_c, w_grp, p_scale)
            yc_win, yc_glob = _context_attention(sink, wq_c, wk_c, wv_c, gq_c, gk_c, gv_c)
            c1, hc2 = _merge(ctx, mod_c, g1, g2, yc_pool, yc_win, yc_glob, wg, bg, wb, wo, tl=tl_c)
            ctx = _ffn(hc2, c1, mod_c, wfg, wfv, cw, cb, wfd, fg, tl=tl_c, final=False)

        x = _ffn(h2, x1, mod_l, wfg, wfv, cw, cb, wfd, fg, tl=tl_ffn, final=last)
    return x
```

```python
import functools

import jax
import jax.numpy as jnp
import numpy as np
from jax import lax
from jax.experimental import pallas as pl
from jax.experimental.pallas import tpu as pltpu

F32 = jnp.float32
BF16 = jnp.bfloat16

GRID_W = 64
HEAD_DIM = 64
Q_PER_KV = 4
POOL_WINDOWS = (2, 4, 8, 16)
POOL_HALO = 16
WINDOW = 128
BLOCK = 128
ROPE_THETA = 10000.0
EPS = 1e-6
NEG = -1e30
SM_SCALE = HEAD_DIM ** -0.5
LOG2E = 1.4426950408889634
Q_SCALE = SM_SCALE * LOG2E
BOUND_SLACK = 1.02
BOUND_SHIFT = 60.0
BOUND_MAX = 80.0
LANES = 128
VT_ROWS = 80
BF16_SUBLANES = 16
VMEM_LIMIT = 56 * 1024 * 1024


def _cparams(n_axes):
    return pltpu.CompilerParams(
        dimension_semantics=("arbitrary",) * n_axes,
        vmem_limit_bytes=VMEM_LIMIT,
    )


def _resident(a):
    return pl.BlockSpec(a.shape, lambda *_: (0,) * a.ndim, pipeline_mode=pl.Buffered(1))


def _rms_mod(x, g, shift, scale):
    y = x * lax.rsqrt(jnp.mean(x * x, axis=-1, keepdims=True) + EPS)
    return (y * g) * (1.0 + scale) + shift


def _dot(a, b):
    return jnp.dot(a, b, preferred_element_type=F32)


def _dot_nt(a, b):
    return lax.dot_general(a, b, (((1,), (1,)), ((), ())), preferred_element_type=F32)


def _mod_kernel(c_ref, w_ref, b_ref, o_ref):
    c = c_ref[...]
    s = c * jax.nn.sigmoid(c)
    o_ref[0] = _dot(s.astype(BF16), w_ref[0].astype(BF16)) + b_ref[0]


def _modulation(cc, w_mod, b_mod):
    depth, d, n = w_mod.shape
    rows = cc.shape[0]
    tn = 1536
    return pl.pallas_call(
        _mod_kernel,
        out_shape=jax.ShapeDtypeStruct((depth, rows, n), F32),
        grid=(depth, n // tn),
        in_specs=[
            pl.BlockSpec((rows, d), lambda l, j: (0, 0)),
            pl.BlockSpec((1, d, tn), lambda l, j: (l, 0, j)),
            pl.BlockSpec((1, 1, tn), lambda l, j: (l, 0, j)),
        ],
        out_specs=pl.BlockSpec((1, rows, tn), lambda l, j: (l, 0, j)),
        compiler_params=_cparams(2),
        name="modulation",
    )(cc, w_mod, b_mod.reshape(depth, 1, n))


def _head_rms(t, bd, g):
    sq = t * t
    hi = sq.astype(BF16)
    lo = (sq - hi.astype(F32)).astype(BF16)
    w = t.shape[1]
    parts = []
    for j in range(0, w, 2 * LANES):
        e = min(j + 2 * LANES, w)
        b = bd[: e - j, : e - j]
        parts.append(_dot(hi[:, j:e], b) + _dot(lo[:, j:e], b))
    ssq = parts[0] if len(parts) == 1 else jnp.concatenate(parts, axis=1)
    return t * lax.rsqrt(ssq * (1.0 / HEAD_DIM) + EPS) * g


def _rope(t, cos, sin, partner_lane):
    outs = []
    for j in range(0, t.shape[1], LANES):
        tj = t[:, j:j + LANES]
        partner = jnp.take_along_axis(tj, partner_lane, axis=1)
        outs.append(tj * cos + partner * sin)
    return outs[0] if len(outs) == 1 else jnp.concatenate(outs, axis=1)


def _inproj_kernel(x_ref, mod_ref, g1_ref, w_ref, cos_ref, sin_ref, qg_ref, kg_ref, bd_ref,
                   u_ref, wq_ref, wk_ref, wv_ref, gq_ref, gk_ref, gv_ref, *, rope, pool_w, attn_w, kv_w):
    x = x_ref[0]
    h = _rms_mod(x, g1_ref[...], mod_ref[0, 0:1, :], mod_ref[0, 1:2, :])
    h = h.astype(BF16)
    mix_w = attn_w + 2 * kv_w
    zg = _dot(h, w_ref[:, pool_w + mix_w:pool_w + 2 * mix_w])
    zw = _dot(h, w_ref[:, pool_w:pool_w + mix_w])
    u_ref[0] = _dot(h, w_ref[:, :pool_w])
    gq, gk, gv = zg[:, :attn_w], zg[:, attn_w:attn_w + kv_w], zg[:, attn_w + kv_w:]
    wq, wk, wv = zw[:, :attn_w], zw[:, attn_w:attn_w + kv_w], zw[:, attn_w + kv_w:]
    bd = bd_ref[...]
    gq = _head_rms(gq, bd, qg_ref[...])
    gk = _head_rms(gk, bd, kg_ref[...])
    if rope:
        cos = cos_ref[...]
        sin = sin_ref[...]
        lane = lax.broadcasted_iota(jnp.int32, cos.shape, 1)
        partner_lane = lane ^ 16
        wq = _rope(wq, cos, sin, partner_lane)
        wk = _rope(wk, cos, sin, partner_lane)
        gq = _rope(gq, cos, sin, partner_lane)
        gk = _rope(gk, cos, sin, partner_lane)
    wq_ref[0] = (wq * Q_SCALE).astype(BF16)
    gq_ref[0] = (gq * Q_SCALE).astype(BF16)
    wv_ref[0] = wv.astype(BF16)
    wk_ref[0] = wk.astype(BF16)
    gvt = gv.T
    tail_row = lax.broadcasted_iota(jnp.int32, (VT_ROWS - HEAD_DIM, gv.shape[0]), 0)
    tail = jnp.where(tail_row == 0, 1.0, 0.0).astype(BF16)
    for k in range(kv_w // HEAD_DIM):
        gk_ref[0, k] = gk[:, k * HEAD_DIM:(k + 1) * HEAD_DIM].astype(BF16)
        gv_ref[0, k, 0:HEAD_DIM, :] = gvt[k * HEAD_DIM:(k + 1) * HEAD_DIM].astype(BF16)
        gv_ref[0, k, HEAD_DIM:VT_ROWS, :] = tail


def _inproj(x, mod, g1, w_in, cos, sin, qg, kg, bd, *, rope, tl):
    bn, ln, d = x.shape
    in_w = w_in.shape[1]
    attn_w = qg.shape[1]
    kv_w = kg.shape[1]
    pool_w = in_w - 2 * attn_w - 4 * kv_w
    n_kv = kv_w // HEAD_DIM
    mod_b = mod.shape[0]
    mod_map = (lambda b, i: (b, 0, 0)) if mod_b > 1 else (lambda b, i: (0, 0, 0))
    tok = lambda w: pl.BlockSpec((1, tl, w), lambda b, i: (b, i, 0))
    full2 = _resident
    kern = functools.partial(_inproj_kernel, rope=rope, pool_w=pool_w, attn_w=attn_w, kv_w=kv_w)
    out_shapes = (
        jax.ShapeDtypeStruct((bn, ln, pool_w), F32),
        jax.ShapeDtypeStruct((bn, ln, attn_w), BF16),
        jax.ShapeDtypeStruct((bn, ln, kv_w), BF16),
        jax.ShapeDtypeStruct((bn, ln, kv_w), BF16),
        jax.ShapeDtypeStruct((bn, ln, attn_w), BF16),
        jax.ShapeDtypeStruct((bn, n_kv, ln, HEAD_DIM), BF16),
        jax.ShapeDtypeStruct((bn, n_kv, VT_ROWS, ln), BF16),
    )
    assert kv_w == LANES
    head_spec = pl.BlockSpec((1, n_kv, tl, HEAD_DIM), lambda b, i: (b, 0, i, 0))
    vt_spec = pl.BlockSpec((1, n_kv, VT_ROWS, tl), lambda b, i: (b, 0, 0, i))
    return pl.pallas_call(
        kern,
        out_shape=out_shapes,
        grid=(bn, ln // tl),
        in_specs=[
            tok(d),
            pl.BlockSpec((1, 6, d), mod_map),
            full2(g1),
            full2(w_in),
            pl.BlockSpec((tl, LANES), lambda b, i: (i, 0)),
            pl.BlockSpec((tl, LANES), lambda b, i: (i, 0)),
            full2(qg),
            full2(kg),
            full2(bd),
        ],
        out_specs=(tok(pool_w), tok(attn_w), tok(kv_w), tok(kv_w), tok(attn_w), head_spec, vt_spec),
        compiler_params=_cparams(2),
        name="inproj_rope" if rope else "inproj_ctx",
    )(x, mod, g1, w_in, cos, sin, qg, kg, bd)


def _pool_kernel(u_ref, w_ref, s_ref, o_ref, ext_ref, *, ln):
    n_grp = len(POOL_WINDOWS)
    gw = u_ref.shape[2] // n_grp
    t = lax.broadcasted_iota(jnp.int32, (ln, 1), 0)
    ext_len = ln + 2 * POOL_HALO
    zeros = jnp.zeros((POOL_HALO, gw), F32)
    for gi, win in enumerate(POOL_WINDOWS):
        u = u_ref[0, :, gi * gw:(gi + 1) * gw]
        ext_ref[0:POOL_HALO, :] = zeros
        ext_ref[POOL_HALO + ln:ext_len, :] = zeros
        ext_ref[POOL_HALO:POOL_HALO + ln, :] = u
        e = ext_ref[...]
        acc = e + pltpu.roll(e, 1, axis=0)
        half = 1
        while 2 * half < win:
            acc = pltpu.roll(acc, half, axis=0) + pltpu.roll(acc, ext_len - half, axis=0)
            half *= 2
        wsum = acc[POOL_HALO:POOL_HALO + ln, :]
        cnt = (jnp.minimum(t + win // 2, ln) - jnp.maximum(t - win // 2, 0)).astype(F32)
        p = wsum / cnt - u
        y = _dot(p.astype(BF16), w_ref[gi]) * s_ref[:, gi * gw:(gi + 1) * gw]
        o_ref[0, :, gi * gw:(gi + 1) * gw] = y.astype(BF16)


def _pool(u, w_grp, scale):
    bn, ln, pw = u.shape
    gw = pw // len(POOL_WINDOWS)
    return pl.pallas_call(
        functools.partial(_pool_kernel, ln=ln),
        out_shape=jax.ShapeDtypeStruct((bn, ln, pw), BF16),
        grid=(bn,),
        in_specs=[
            pl.BlockSpec((1, ln, pw), lambda b: (b, 0, 0)),
            pl.BlockSpec(w_grp.shape, lambda b: (0, 0, 0)),
            pl.BlockSpec(scale.shape, lambda b: (0, 0)),
        ],
        out_specs=pl.BlockSpec((1, ln, pw), lambda b: (b, 0, 0)),
        scratch_shapes=[pltpu.VMEM((ln + 2 * POOL_HALO, gw), F32)],
        compiler_params=_cparams(1),
        name="pool_mixer",
    )(u, w_grp, scale)


def _stack_heads(q, kv):
    return jnp.concatenate(
        [q[:, (kv * Q_PER_KV + g) * HEAD_DIM:(kv * Q_PER_KV + g + 1) * HEAD_DIM] for g in range(Q_PER_KV)], axis=0)


def _rows_to_tokens(ot):
    pad = LANES - ot.shape[0]
    if pad:
        ot = jnp.concatenate([ot, jnp.zeros((pad, ot.shape[1]), F32)], axis=0)
    return ot.T[:, :HEAD_DIM]


def _unstack_heads(o, t):
    return [o[g * t:(g + 1) * t] for g in range(Q_PER_KV)]


def _sink_column(sink_ref, kv, t):
    row = lax.broadcasted_iota(jnp.int32, (Q_PER_KV * t, 1), 0)
    col = jnp.full((Q_PER_KV * t, 1), sink_ref[kv * Q_PER_KV + Q_PER_KV - 1] * LOG2E, F32)
    for g in range(Q_PER_KV - 2, -1, -1):
        col = jnp.where(row < (g + 1) * t, sink_ref[kv * Q_PER_KV + g] * LOG2E, col)
    return col


def _ones_column(n):
    return jnp.ones((n, LANES), BF16)


def _win_kernel(sink_ref, q_ref, kp_ref, kc_ref, kn_ref, vp_ref, vc_ref, vn_ref, kx_ref, vx_ref, o_ref,
                pl_ref, px_ref, es_ref, *, qb, ni):
    t = pl.program_id(0)
    i = jnp.minimum(t, pl.num_programs(0) - 2) % ni
    kall = jnp.concatenate([kp_ref[0], kc_ref[0], kn_ref[0]], axis=0)
    vall = jnp.concatenate([vp_ref[0], vc_ref[0], vn_ref[0]], axis=0)
    kx = kx_ref[0]
    vall_aug = jnp.concatenate([vall, _ones_column(vall.shape[0])], axis=1)
    vx_aug = jnp.concatenate([vx_ref[0], _ones_column(kx.shape[0])], axis=1)
    kw = kall.shape[1]
    n_kv = kw // HEAD_DIM
    row = lax.broadcasted_iota(jnp.int32, (BLOCK, 3 * BLOCK), 0)
    col = lax.broadcasted_iota(jnp.int32, (BLOCK, 3 * BLOCK), 1)
    band = jnp.where(jnp.abs(col - BLOCK - row) <= WINDOW, 0.0, NEG)
    no_prev = jnp.where((col < BLOCK) & (i == 0), NEG, 0.0)
    no_next = jnp.where((col >= 2 * BLOCK) & (i == ni - 1), NEG, 0.0)

    @pl.when(t == 0)
    def _():
        pl_ref[...] = jnp.zeros_like(pl_ref)
        px_ref[...] = jnp.zeros_like(px_ref)
        es_ref[...] = jnp.ones_like(es_ref)

    blocks = []
    for j in range(qb):
        bias = band
        if j == 0:
            bias = bias + no_prev
        if j == qb - 1:
            bias = bias + no_next
        bias = jnp.concatenate([bias] * Q_PER_KV, axis=0)
        q = q_ref[0, j * BLOCK:(j + 1) * BLOCK, :]
        kwin = kall[j * BLOCK:(j + 3) * BLOCK]
        vwin_aug = vall_aug[j * BLOCK:(j + 3) * BLOCK]
        outs = []
        for kv in range(n_kv):
            g = j * n_kv + kv
            sl = slice(kv * HEAD_DIM, (kv + 1) * HEAD_DIM)
            oa = _dot(pl_ref[g], vwin_aug) + _dot(px_ref[g], vx_aug)
            l = oa[:, kw:] + es_ref[g]
            outs += _unstack_heads((oa[:, :kw] / l)[:, sl], BLOCK)
            q4 = _stack_heads(q, kv)
            s_lat = _dot_nt(q4, kwin[:, sl]) + bias
            s_ctx = _dot_nt(q4, kx[:, sl])
            sink = _sink_column(sink_ref, kv, BLOCK)
            m = jnp.maximum(jnp.concatenate([s_lat, s_ctx], axis=1).max(axis=-1, keepdims=True), sink)
            pl_ref[g] = jnp.exp2(s_lat - m).astype(BF16)
            px_ref[g] = jnp.exp2(s_ctx - m).astype(BF16)
            es_ref[g] = jnp.broadcast_to(jnp.exp2(sink - m), es_ref.shape[1:])
        blocks.append(jnp.concatenate(outs, axis=1).astype(BF16))
    o_ref[0] = blocks[0] if qb == 1 else jnp.concatenate(blocks, axis=0)


def _window_attention(sink, q, k, v, kx, vx, *, qb):
    bn, ln, aw = q.shape
    kw = k.shape[2]
    cn = kx.shape[1]
    nb = ln // BLOCK
    assert kw == LANES and nb % qb == 0
    ni = nb // qb
    n_tiles = bn * ni
    n_grp = qb * (kw // HEAD_DIM)

    def qk_tile(t):
        t = jnp.minimum(t, n_tiles - 1)
        return t // ni, t % ni

    def pv_tile(t):
        t = jnp.maximum(t - 1, 0)
        return t // ni, t % ni

    def maps(tile):
        cur = lambda t: (tile(t)[0], tile(t)[1], 0)
        prev = lambda t: (tile(t)[0], jnp.maximum(tile(t)[1] * qb - 1, 0), 0)
        nxt = lambda t: (tile(t)[0], jnp.minimum((tile(t)[1] + 1) * qb, nb - 1), 0)
        ctx = lambda t: (tile(t)[0], 0, 0)
        return cur, prev, nxt, ctx

    edge = lambda f: pl.BlockSpec((1, BLOCK, kw), f)
    mid = lambda f: pl.BlockSpec((1, qb * BLOCK, kw), f)
    ctxb = lambda f: pl.BlockSpec((1, cn, kw), f)
    k_cur, k_prev, k_nxt, k_ctx = maps(qk_tile)
    v_cur, v_prev, v_nxt, v_ctx = maps(pv_tile)
    return pl.pallas_call(
        functools.partial(_win_kernel, qb=qb, ni=ni),
        out_shape=jax.ShapeDtypeStruct((bn, ln, aw), BF16),
        grid=(n_tiles + 1,),
        in_specs=[
            pl.BlockSpec(memory_space=pltpu.SMEM),
            pl.BlockSpec((1, qb * BLOCK, aw), k_cur),
            edge(k_prev), mid(k_cur), edge(k_nxt),
            edge(v_prev), mid(v_cur), edge(v_nxt),
            ctxb(k_ctx), ctxb(v_ctx),
        ],
        out_specs=pl.BlockSpec((1, qb * BLOCK, aw), v_cur),
        scratch_shapes=[
            pltpu.VMEM((n_grp, Q_PER_KV * BLOCK, 3 * BLOCK), BF16),
            pltpu.VMEM((n_grp, Q_PER_KV * BLOCK, cn), BF16),
            pltpu.VMEM((n_grp, Q_PER_KV * BLOCK, LANES), F32),
        ],
        compiler_params=_cparams(1),
        name="window_attention",
    )(sink, q, k, k, k, v, v, v, kx, vx)


def _glob_kernel(bound_ref, q_ref, k_ref, v_ref, kx_ref, vx_ref, o_ref, m_ref, p_ref, *, kc):
    s = pl.program_id(0)
    ln = k_ref.shape[2]
    cn = kx_ref.shape[2]
    q = q_ref[0]
    tq = q.shape[0]
    q4 = _stack_heads(q, 0)
    pieces = [(k_ref, v_ref, c0, min(kc, ln - c0), c0) for c0 in range(0, ln, kc)] + [(kx_ref, vx_ref, 0, cn, ln)]

    @pl.when(s == 0)
    def _():
        p_ref[...] = jnp.zeros_like(p_ref)

    bound = bound_ref[0]
    m_ref[...] = jnp.full(m_ref.shape, bound - BOUND_SHIFT, F32)

    @pl.when(bound > BOUND_MAX)
    def _():
        m = None
        for kr, _, c0, n, _ in pieces:
            mc = _dot_nt(kr[0, 0, c0:c0 + n, :], q4).max(axis=0, keepdims=True)
            m = mc if m is None else jnp.maximum(m, mc)
        m_ref[...] = m

    m = m_ref[...]
    acc = None
    for kr, vr, c0, n, off in pieces:
        oc = _dot(vr[0, 0, :, c0:c0 + n], p_ref[off:off + n, :])
        acc = oc if acc is None else acc + oc
        p_ref[off:off + n, :] = jnp.exp2(_dot_nt(kr[0, 0, c0:c0 + n, :], q4) - m).astype(BF16)
    l = acc[HEAD_DIM:HEAD_DIM + 1, :]
    l = jnp.where(l > 0.0, l, 1.0)
    o = _rows_to_tokens(acc / l)
    o_ref[0] = jnp.concatenate(_unstack_heads(o, tq), axis=1).astype(BF16)


def _score_bound(qg, kg):
    return (HEAD_DIM * Q_SCALE * BOUND_SLACK * jnp.max(jnp.abs(qg)) * jnp.max(jnp.abs(kg))).reshape(1)


def _global_attention(bound, q, k, v, kx, vx, *, tq, kc):
    bn, ln, aw = q.shape
    n_kv = k.shape[1]
    cn = kx.shape[2]
    gw = aw // n_kv
    assert gw == Q_PER_KV * HEAD_DIM
    nq = ln // tq
    n_tiles = bn * n_kv * nq

    def split(t):
        return t // (n_kv * nq), (t // nq) % n_kv, t % nq

    def qk_tile(s):
        return split(jnp.minimum(s, n_tiles - 1))

    def pv_tile(s):
        return split(jnp.maximum(s - 1, 0))

    def q_map(s):
        b, h, i = qk_tile(s)
        return b, i, h

    def k_map(s):
        b, h, _ = qk_tile(s)
        return b, h, 0, 0

    def v_map(s):
        b, h, _ = pv_tile(s)
        return b, h, 0, 0

    def o_map(s):
        b, h, i = pv_tile(s)
        return b, i, h

    return pl.pallas_call(
        functools.partial(_glob_kernel, kc=kc),
        out_shape=jax.ShapeDtypeStruct((bn, ln, aw), BF16),
        grid=(n_tiles + 1,),
        in_specs=[
            pl.BlockSpec(memory_space=pltpu.SMEM),
            pl.BlockSpec((1, tq, gw), q_map),
            pl.BlockSpec((1, 1, ln, HEAD_DIM), k_map),
            pl.BlockSpec((1, 1, VT_ROWS, ln), v_map),
            pl.BlockSpec((1, 1, cn, HEAD_DIM), k_map),
            pl.BlockSpec((1, 1, VT_ROWS, cn), v_map),
        ],
        out_specs=pl.BlockSpec((1, tq, gw), o_map),
        scratch_shapes=[pltpu.VMEM((1, Q_PER_KV * tq), F32), pltpu.VMEM((ln + cn, Q_PER_KV * tq), BF16)],
        compiler_params=_cparams(1),
        name="global_attention",
    )(bound, q, k, v, kx, vx)


def _ctx_attn_kernel(sink_ref, wq_ref, wk_ref, wv_ref, gq_ref, gk_ref, gv_ref, ow_ref, og_ref):
    wq = wq_ref[0]
    gq = gq_ref[0]
    wk = wk_ref[0]
    cn, kw = wk.shape
    wv_aug = jnp.concatenate([wv_ref[0], _ones_column(cn)], axis=1)
    outs_w = []
    outs_g = []
    for kv in range(kw // HEAD_DIM):
        sl = slice(kv * HEAD_DIM, (kv + 1) * HEAD_DIM)
        s = _dot_nt(_stack_heads(wq, kv), wk[:, sl])
        sink = _sink_column(sink_ref, kv, cn)
        m = jnp.maximum(s.max(axis=-1, keepdims=True), sink)
        oa = _dot(jnp.exp2(s - m).astype(BF16), wv_aug)
        l = oa[:, kw:] + jnp.exp2(sink - m)
        outs_w += _unstack_heads((oa[:, :kw] / l)[:, sl], cn)
        st = _dot_nt(gk_ref[0, kv], _stack_heads(gq, kv))
        m = st.max(axis=0, keepdims=True)
        oa = _dot(gv_ref[0, kv], jnp.exp2(st - m).astype(BF16))
        outs_g += _unstack_heads(_rows_to_tokens(oa / oa[HEAD_DIM:HEAD_DIM + 1, :]), cn)
    ow_ref[0] = jnp.concatenate(outs_w, axis=1).astype(BF16)
    og_ref[0] = jnp.concatenate(outs_g, axis=1).astype(BF16)


def _context_attention(sink, wq, wk, wv, gq, gk, gv):
    bn, cn, aw = wq.shape
    kw = wk.shape[2]
    n_kv = gk.shape[1]
    tok = lambda w: pl.BlockSpec((1, cn, w), lambda b: (b, 0, 0))
    hd = lambda r, w: pl.BlockSpec((1, n_kv, r, w), lambda b: (b, 0, 0, 0))
    shp = jax.ShapeDtypeStruct((bn, cn, aw), BF16)
    return pl.pallas_call(
        _ctx_attn_kernel,
        out_shape=(shp, shp),
        grid=(bn,),
        in_specs=[pl.BlockSpec(memory_space=pltpu.SMEM), tok(aw), tok(kw), tok(kw), tok(aw),
                  hd(cn, HEAD_DIM), hd(VT_ROWS, cn)],
        out_specs=(tok(aw), tok(aw)),
        compiler_params=_cparams(1),
        name="context_attention",
    )(sink, wq, wk, wv, gq, gk, gv)


def _merge_kernel(x_ref, mod_ref, g1_ref, g2_ref, yp_ref, yw_ref, yg_ref, wg_ref, bg_ref, wb_ref, wo_ref,
                  x1_ref, h2_ref):
    x = x_ref[0]
    h = _rms_mod(x, g1_ref[...], mod_ref[0, 0:1, :], mod_ref[0, 1:2, :]).astype(BF16)
    merged = None
    for i, y_ref in enumerate((yp_ref, yw_ref, yg_ref)):
        gate = jax.nn.sigmoid(_dot(h, wg_ref[i]) + bg_ref[i:i + 1, :])
        term = gate * _dot(y_ref[0], wb_ref[i])
        merged = term if merged is None else merged + term
    x1 = x + mod_ref[0, 2:3, :] * _dot(merged.astype(BF16), wo_ref[...])
    x1_ref[0] = x1
    h2_ref[0] = _rms_mod(x1, g2_ref[...], mod_ref[0, 3:4, :], mod_ref[0, 4:5, :]).astype(BF16)


def _merge(x, mod, g1, g2, yp, yw, yg, wg, bg, wb, wo, *, tl):
    bn, ln, d = x.shape
    bw = yp.shape[2]
    mod_map = (lambda b, i: (b, 0, 0)) if mod.shape[0] > 1 else (lambda b, i: (0, 0, 0))
    tok = lambda w: pl.BlockSpec((1, tl, w), lambda b, i: (b, i, 0))
    full = _resident
    return pl.pallas_call(
        _merge_kernel,
        out_shape=(jax.ShapeDtypeStruct((bn, ln, d), F32), jax.ShapeDtypeStruct((bn, ln, d), BF16)),
        grid=(bn, ln // tl),
        in_specs=[tok(d), pl.BlockSpec((1, 6, d), mod_map), full(g1), full(g2), tok(bw), tok(bw), tok(bw),
                  full(wg), full(bg), full(wb), full(wo)],
        out_specs=(tok(d), tok(d)),
        compiler_params=_cparams(2),
        name="merge_branches",
    )(x, mod, g1, g2, yp, yw, yg, wg, bg, wb, wo)


def _ffn_kernel(h_ref, hp_ref, hn_ref, x_ref, mod_ref, wg_ref, wv_ref, cw_ref, cb_ref, wd_ref, fg_ref,
                o_ref, g_scr, *, final):
    j = pl.program_id(1)
    nj = pl.num_programs(1)
    tl = h_ref.shape[1]
    halo = hp_ref.shape[1]
    hm = h_ref[0]
    hp = jnp.where(j > 0, hp_ref[0], jnp.zeros_like(hp_ref[0]))
    hn = jnp.where(j < nj - 1, hn_ref[0], jnp.zeros_like(hn_ref[0]))
    hext = jnp.concatenate([hp, hm, hn], axis=0)
    g_scr[...] = _dot(hext, wg_ref[...])
    a = (g_scr[pl.ds(halo - 1, tl), :] * cw_ref[0:1, :]
         + g_scr[pl.ds(halo, tl), :] * cw_ref[1:2, :]
         + g_scr[pl.ds(halo + 1, tl), :] * cw_ref[2:3, :]
         + cb_ref[...])
    val = _dot(hm, wv_ref[...])
    act = (a * jax.nn.sigmoid(a)) * val
    x2 = x_ref[0] + mod_ref[0, 5:6, :] * _dot(act.astype(BF16), wd_ref[...])
    if final:
        x2 = x2 * lax.rsqrt(jnp.mean(x2 * x2, axis=-1, keepdims=True) + EPS) * fg_ref[...]
    o_ref[0] = x2


def _ffn(h2, x1, mod, wg, wv, cw, cb, wd, fg, *, tl, final):
    bn, ln, d = x1.shape
    f = wg.shape[1]
    halo = BF16_SUBLANES
    r = tl // halo
    nh = ln // halo
    mod_map = (lambda b, i: (b, 0, 0)) if mod.shape[0] > 1 else (lambda b, i: (0, 0, 0))
    tok = pl.BlockSpec((1, tl, d), lambda b, i: (b, i, 0))
    full = _resident
    return pl.pallas_call(
        functools.partial(_ffn_kernel, final=final),
        out_shape=jax.ShapeDtypeStruct((bn, ln, d), F32),
        grid=(bn, ln // tl),
        in_specs=[
            tok,
            pl.BlockSpec((1, halo, d), lambda b, i: (b, jnp.maximum(i * r - 1, 0), 0)),
            pl.BlockSpec((1, halo, d), lambda b, i: (b, jnp.minimum((i + 1) * r, nh - 1), 0)),
            tok,
            pl.BlockSpec((1, 6, d), mod_map),
            full(wg), full(wv), full(cw), full(cb), full(wd), full(fg),
        ],
        out_specs=tok,
        scratch_shapes=[pltpu.VMEM((tl + 2 * halo, f), F32)],
        compiler_params=_cparams(2),
        name="conv_glu_final" if final else "conv_glu",
    )(h2, h2, h2, x1, mod, wg, wv, cw, cb, wd, fg)


def _rope_tables(n_tok):
    pos = jnp.arange(n_tok)
    row = (pos // GRID_W).astype(F32)
    col = (pos % GRID_W).astype(F32)
    half = HEAD_DIM // 2
    inv = ROPE_THETA ** (-jnp.arange(0, half, 2, dtype=F32) / half)
    ang_r = row[:, None] * inv
    ang_c = col[:, None] * inv
    cos64 = jnp.concatenate([jnp.cos(ang_r)] * 2 + [jnp.cos(ang_c)] * 2, axis=1)
    sin64 = jnp.concatenate([-jnp.sin(ang_r), jnp.sin(ang_r), -jnp.sin(ang_c), jnp.sin(ang_c)], axis=1)
    reps = LANES // HEAD_DIM
    return jnp.tile(cos64, (1, reps)), jnp.tile(sin64, (1, reps))


def _tile_rows(n, target):
    t = min(n, target)
    while n % t:
        t //= 2
    return t


def kernel(x, c, ctx, c_ctx, w_mod, b_mod, norm1_g, norm2_g, w_in, w_pool_grp, pool_scale, win_sink,
           q_norm_g, k_norm_g, w_branch, w_gate, b_gate, w_out, w_ff_gate, w_ff_val, conv_w, conv_b,
           w_ff_down, final_g):
    bn, ln, d = x.shape
    cn = ctx.shape[1]
    depth = w_mod.shape[0]
    attn_w = d // 2
    kv_w = attn_w // Q_PER_KV
    n_heads = attn_w // HEAD_DIM

    mod_all = _modulation(jnp.concatenate([c, c_ctx[None]], axis=0), w_mod, b_mod)
    mod_all = mod_all.reshape(depth, bn + 1, 6, d)

    cos, sin = _rope_tables(ln)
    cos_c = jnp.ones((cn, LANES), F32)
    sin_c = jnp.zeros((cn, LANES), F32)
    ids = np.arange(2 * LANES) // HEAD_DIM
    bd = jnp.asarray(ids[:, None] == ids[None, :], dtype=BF16)
    fg = final_g.reshape(1, d)

    tl = _tile_rows(ln, 1024)
    tl_ffn = _tile_rows(ln, 512)
    tl_c = _tile_rows(cn, 256)
    tq = _tile_rows(ln, 512)
    win_qb = _tile_rows(ln // BLOCK, 4)

    for l in range(depth):
        last = l == depth - 1
        mod_l = mod_all[l, :bn]
        mod_c = mod_all[l, bn:]
        g1 = norm1_g[l].reshape(1, d)
        g2 = norm2_g[l].reshape(1, d)
        w_in_l = w_in[l].astype(BF16)
        qg = jnp.tile(q_norm_g[l], attn_w // HEAD_DIM).reshape(1, attn_w)
        kg = jnp.tile(k_norm_g[l], kv_w // HEAD_DIM).reshape(1, kv_w)
        w_grp = w_pool_grp[l].astype(BF16)
        p_scale = pool_scale[l].reshape(1, -1)
        sink = win_sink[l]
        wg = w_gate[l].astype(BF16)
        bg = b_gate[l]
        wb = w_branch[l].astype(BF16)
        wo = w_out[l].astype(BF16)
        wfg = w_ff_gate[l].astype(BF16)
        wfv = w_ff_val[l].astype(BF16)
        wfd = w_ff_down[l].astype(BF16)
        cw = conv_w[l]
        cb = conv_b[l].reshape(1, -1)

        u_l, wq_l, wk_l, wv_l, gq_l, gk_l, gv_l = _inproj(
            x, mod_l, g1, w_in_l, cos, sin, qg, kg, bd, rope=True, tl=tl)
        u_c, wq_c, wk_c, wv_c, gq_c, gk_c, gv_c = _inproj(
            ctx, mod_c, g1, w_in_l, cos_c, sin_c, qg, kg, bd, rope=False, tl=tl_c)

        y_pool = _pool(u_l, w_grp, p_scale)
        y_win = _window_attention(sink, wq_l, wk_l, wv_l, wk_c, wv_c, qb=win_qb)
        bound = _score_bound(q_norm_g[l], k_norm_g[l])
        y_glob = _global_attention(bound, gq_l, gk_l, gv_l, gk_c, gv_c, tq=tq, kc=512)
        x1, h2 = _merge(x, mod_l, g1, g2, y_pool, y_win, y_glob, wg, bg, wb, wo, tl=tl)

        if not last:
            yc_pool = _pool(u_c, w_grp, p_scale)
            yc_win, yc_glob = _context_attention(sink, wq_c, wk_c, wv_c, gq_c, gk_c, gv_c)
            c1, hc2 = _merge(ctx, mod_c, g1, g2, yc_pool, yc_win, yc_glob, wg, bg, wb, wo, tl=tl_c)
            ctx = _ffn(hc2, c1, mod_c, wfg, wfv, cw, cb, wfd, fg, tl=tl_c, final=False)

        x = _ffn(h2, x1, mod_l, wfg, wfv, cw, cb, wfd, fg, tl=tl_ffn, final=last)
    return x
```

```python
import functools

import jax
import jax.numpy as jnp
import numpy as np
from jax import lax
from jax.experimental import pallas as pl
from jax.experimental.pallas import tpu as pltpu

F32 = jnp.float32
BF16 = jnp.bfloat16

GRID_W = 64
HEAD_DIM = 64
Q_PER_KV = 4
POOL_WINDOWS = (2, 4, 8, 16)
POOL_HALO = 16
WINDOW = 128
BLOCK = 128
ROPE_THETA = 10000.0
EPS = 1e-6
NEG = -1e30
SM_SCALE = HEAD_DIM ** -0.5
LOG2E = 1.4426950408889634
Q_SCALE = SM_SCALE * LOG2E
BOUND_SLACK = 1.02
BOUND_SHIFT = 60.0
BOUND_MAX = 80.0
LANES = 128
VT_ROWS = 80
BF16_SUBLANES = 16
VMEM_LIMIT = 56 * 1024 * 1024


def _cparams(n_axes):
    return pltpu.CompilerParams(
        dimension_semantics=("arbitrary",) * n_axes,
        vmem_limit_bytes=VMEM_LIMIT,
    )


def _resident(a):
    return pl.BlockSpec(a.shape, lambda *_: (0,) * a.ndim, pipeline_mode=pl.Buffered(1))


def _rms_mod(x, g, shift, scale):
    y = x * lax.rsqrt(jnp.mean(x * x, axis=-1, keepdims=True) + EPS)
    return (y * g) * (1.0 + scale) + shift


def _dot(a, b):
    return jnp.dot(a, b, preferred_element_type=F32)


def _dot_nt(a, b):
    return lax.dot_general(a, b, (((1,), (1,)), ((), ())), preferred_element_type=F32)


def _mod_kernel(c_ref, w_ref, b_ref, o_ref):
    c = c_ref[...]
    s = c * jax.nn.sigmoid(c)
    o_ref[0] = _dot(s.astype(BF16), w_ref[0].astype(BF16)) + b_ref[0]


def _modulation(cc, w_mod, b_mod):
    depth, d, n = w_mod.shape
    rows = cc.shape[0]
    tn = 1536
    return pl.pallas_call(
        _mod_kernel,
        out_shape=jax.ShapeDtypeStruct((depth, rows, n), F32),
        grid=(depth, n // tn),
        in_specs=[
            pl.BlockSpec((rows, d), lambda l, j: (0, 0)),
            pl.BlockSpec((1, d, tn), lambda l, j: (l, 0, j)),
            pl.BlockSpec((1, 1, tn), lambda l, j: (l, 0, j)),
        ],
        out_specs=pl.BlockSpec((1, rows, tn), lambda l, j: (l, 0, j)),
        compiler_params=_cparams(2),
        name="modulation",
    )(cc, w_mod, b_mod.reshape(depth, 1, n))


def _head_rms(t, bd, g):
    sq = t * t
    hi = sq.astype(BF16)
    lo = (sq - hi.astype(F32)).astype(BF16)
    w = t.shape[1]
    parts = []
    for j in range(0, w, 2 * LANES):
        e = min(j + 2 * LANES, w)
        b = bd[: e - j, : e - j]
        parts.append(_dot(hi[:, j:e], b) + _dot(lo[:, j:e], b))
    ssq = parts[0] if len(parts) == 1 else jnp.concatenate(parts, axis=1)
    return t * lax.rsqrt(ssq * (1.0 / HEAD_DIM) + EPS) * g


def _rope(t, cos, sin, partner_lane):
    outs = []
    for j in range(0, t.shape[1], LANES):
        tj = t[:, j:j + LANES]
        partner = jnp.take_along_axis(tj, partner_lane, axis=1)
        outs.append(tj * cos + partner * sin)
    return outs[0] if len(outs) == 1 else jnp.concatenate(outs, axis=1)


def _inproj_kernel(x_ref, mod_ref, g1_ref, w_ref, cos_ref, sin_ref, qg_ref, kg_ref, bd_ref,
                   u_ref, wq_ref, wk_ref, wv_ref, gq_ref, gk_ref, gv_ref, *, rope, pool_w, attn_w, kv_w):
    x = x_ref[0]
    h = _rms_mod(x, g1_ref[...], mod_ref[0, 0:1, :], mod_ref[0, 1:2, :])
    h = h.astype(BF16)
    mix_w = attn_w + 2 * kv_w
    zg = _dot(h, w_ref[:, pool_w + mix_w:pool_w + 2 * mix_w])
    zw = _dot(h, w_ref[:, pool_w:pool_w + mix_w])
    u_ref[0] = _dot(h, w_ref[:, :pool_w])
    gq, gk, gv = zg[:, :attn_w], zg[:, attn_w:attn_w + kv_w], zg[:, attn_w + kv_w:]
    wq, wk, wv = zw[:, :attn_w], zw[:, attn_w:attn_w + kv_w], zw[:, attn_w + kv_w:]
    bd = bd_ref[...]
    gq = _head_rms(gq, bd, qg_ref[...])
    gk = _head_rms(gk, bd, kg_ref[...])
    if rope:
        cos = cos_ref[...]
        sin = sin_ref[...]
        lane = lax.broadcasted_iota(jnp.int32, cos.shape, 1)
        partner_lane = lane ^ 16
        wq = _rope(wq, cos, sin, partner_lane)
        wk = _rope(wk, cos, sin, partner_lane)
        gq = _rope(gq, cos, sin, partner_lane)
        gk = _rope(gk, cos, sin, partner_lane)
    wq_ref[0] = (wq * Q_SCALE).astype(BF16)
    gq_ref[0] = (gq * Q_SCALE).astype(BF16)
    wk_ref[0] = wk.astype(BF16)
    tail_row = lax.broadcasted_iota(jnp.int32, (VT_ROWS - HEAD_DIM, gv.shape[0]), 0)
    tail = jnp.where(tail_row == 0, 1.0, 0.0).astype(BF16)
    for v, v_ref in ((wv, wv_ref), (gv, gv_ref)):
        vt = v.T
        for k in range(kv_w // HEAD_DIM):
            v_ref[0, k, 0:HEAD_DIM, :] = vt[k * HEAD_DIM:(k + 1) * HEAD_DIM].astype(BF16)
            v_ref[0, k, HEAD_DIM:VT_ROWS, :] = tail
    for k in range(kv_w // HEAD_DIM):
        gk_ref[0, k] = gk[:, k * HEAD_DIM:(k + 1) * HEAD_DIM].astype(BF16)


def _inproj(x, mod, g1, w_in, cos, sin, qg, kg, bd, *, rope, tl):
    bn, ln, d = x.shape
    in_w = w_in.shape[1]
    attn_w = qg.shape[1]
    kv_w = kg.shape[1]
    pool_w = in_w - 2 * attn_w - 4 * kv_w
    n_kv = kv_w // HEAD_DIM
    mod_b = mod.shape[0]
    mod_map = (lambda b, i: (b, 0, 0)) if mod_b > 1 else (lambda b, i: (0, 0, 0))
    tok = lambda w: pl.BlockSpec((1, tl, w), lambda b, i: (b, i, 0))
    full2 = _resident
    kern = functools.partial(_inproj_kernel, rope=rope, pool_w=pool_w, attn_w=attn_w, kv_w=kv_w)
    out_shapes = (
        jax.ShapeDtypeStruct((bn, ln, pool_w), F32),
        jax.ShapeDtypeStruct((bn, ln, attn_w), BF16),
        jax.ShapeDtypeStruct((bn, ln, kv_w), BF16),
        jax.ShapeDtypeStruct((bn, n_kv, VT_ROWS, ln), BF16),
        jax.ShapeDtypeStruct((bn, ln, attn_w), BF16),
        jax.ShapeDtypeStruct((bn, n_kv, ln, HEAD_DIM), BF16),
        jax.ShapeDtypeStruct((bn, n_kv, VT_ROWS, ln), BF16),
    )
    assert kv_w == LANES
    head_spec = pl.BlockSpec((1, n_kv, tl, HEAD_DIM), lambda b, i: (b, 0, i, 0))
    vt_spec = pl.BlockSpec((1, n_kv, VT_ROWS, tl), lambda b, i: (b, 0, 0, i))
    return pl.pallas_call(
        kern,
        out_shape=out_shapes,
        grid=(bn, ln // tl),
        in_specs=[
            tok(d),
            pl.BlockSpec((1, 6, d), mod_map),
            full2(g1),
            full2(w_in),
            pl.BlockSpec((tl, LANES), lambda b, i: (i, 0)),
            pl.BlockSpec((tl, LANES), lambda b, i: (i, 0)),
            full2(qg),
            full2(kg),
            full2(bd),
        ],
        out_specs=(tok(pool_w), tok(attn_w), tok(kv_w), vt_spec, tok(attn_w), head_spec, vt_spec),
        compiler_params=_cparams(2),
        name="inproj_rope" if rope else "inproj_ctx",
    )(x, mod, g1, w_in, cos, sin, qg, kg, bd)


def _pool_kernel(u_ref, w_ref, s_ref, o_ref, ext_ref, *, ln):
    n_grp = len(POOL_WINDOWS)
    gw = u_ref.shape[2] // n_grp
    t = lax.broadcasted_iota(jnp.int32, (ln, 1), 0)
    ext_len = ln + 2 * POOL_HALO
    zeros = jnp.zeros((POOL_HALO, gw), F32)
    for gi, win in enumerate(POOL_WINDOWS):
        u = u_ref[0, :, gi * gw:(gi + 1) * gw]
        ext_ref[0:POOL_HALO, :] = zeros
        ext_ref[POOL_HALO + ln:ext_len, :] = zeros
        ext_ref[POOL_HALO:POOL_HALO + ln, :] = u
        e = ext_ref[...]
        acc = e + pltpu.roll(e, 1, axis=0)
        half = 1
        while 2 * half < win:
            acc = pltpu.roll(acc, half, axis=0) + pltpu.roll(acc, ext_len - half, axis=0)
            half *= 2
        wsum = acc[POOL_HALO:POOL_HALO + ln, :]
        cnt = (jnp.minimum(t + win // 2, ln) - jnp.maximum(t - win // 2, 0)).astype(F32)
        p = wsum / cnt - u
        y = _dot(p.astype(BF16), w_ref[gi]) * s_ref[:, gi * gw:(gi + 1) * gw]
        o_ref[0, :, gi * gw:(gi + 1) * gw] = y.astype(BF16)


def _pool(u, w_grp, scale):
    bn, ln, pw = u.shape
    gw = pw // len(POOL_WINDOWS)
    return pl.pallas_call(
        functools.partial(_pool_kernel, ln=ln),
        out_shape=jax.ShapeDtypeStruct((bn, ln, pw), BF16),
        grid=(bn,),
        in_specs=[
            pl.BlockSpec((1, ln, pw), lambda b: (b, 0, 0)),
            pl.BlockSpec(w_grp.shape, lambda b: (0, 0, 0)),
            pl.BlockSpec(scale.shape, lambda b: (0, 0)),
        ],
        out_specs=pl.BlockSpec((1, ln, pw), lambda b: (b, 0, 0)),
        scratch_shapes=[pltpu.VMEM((ln + 2 * POOL_HALO, gw), F32)],
        compiler_params=_cparams(1),
        name="pool_mixer",
    )(u, w_grp, scale)


def _stack_heads(q, kv):
    return jnp.concatenate(
        [q[:, (kv * Q_PER_KV + g) * HEAD_DIM:(kv * Q_PER_KV + g + 1) * HEAD_DIM] for g in range(Q_PER_KV)], axis=0)


def _rows_to_tokens(ot):
    pad = LANES - ot.shape[0]
    if pad:
        ot = jnp.concatenate([ot, jnp.zeros((pad, ot.shape[1]), F32)], axis=0)
    return ot.T[:, :HEAD_DIM]


def _unstack_heads(o, t):
    return [o[g * t:(g + 1) * t] for g in range(Q_PER_KV)]


def _sink_row(sink_ref, kv, t):
    lane = lax.broadcasted_iota(jnp.int32, (1, Q_PER_KV * t), 1)
    row = jnp.full((1, Q_PER_KV * t), sink_ref[kv * Q_PER_KV + Q_PER_KV - 1] * LOG2E, F32)
    for g in range(Q_PER_KV - 2, -1, -1):
        row = jnp.where(lane < (g + 1) * t, sink_ref[kv * Q_PER_KV + g] * LOG2E, row)
    return row


def _win_kernel(sink_ref, q_ref, kp_ref, kc_ref, kn_ref, vp_ref, vc_ref, vn_ref, kx_ref, vx_ref, o_ref,
                pl_ref, px_ref, es_ref, *, qb, ni):
    t = pl.program_id(0)
    i = jnp.minimum(t, pl.num_programs(0) - 2) % ni
    kall = jnp.concatenate([kp_ref[0], kc_ref[0], kn_ref[0]], axis=0)
    vall = jnp.concatenate([vp_ref[0], vc_ref[0], vn_ref[0]], axis=2)
    kx = kx_ref[0]
    n_kv = kall.shape[1] // HEAD_DIM
    key = lax.broadcasted_iota(jnp.int32, (3 * BLOCK, BLOCK), 0)
    qry = lax.broadcasted_iota(jnp.int32, (3 * BLOCK, BLOCK), 1)
    band = jnp.where(jnp.abs(key - BLOCK - qry) <= WINDOW, 0.0, NEG)
    no_prev = jnp.where((key < BLOCK) & (i == 0), NEG, 0.0)
    no_next = jnp.where((key >= 2 * BLOCK) & (i == ni - 1), NEG, 0.0)

    @pl.when(t == 0)
    def _():
        pl_ref[...] = jnp.zeros_like(pl_ref)
        px_ref[...] = jnp.zeros_like(px_ref)
        es_ref[...] = jnp.ones_like(es_ref)

    blocks = []
    for j in range(qb):
        bias = band
        if j == 0:
            bias = bias + no_prev
        if j == qb - 1:
            bias = bias + no_next
        bias = jnp.concatenate([bias] * Q_PER_KV, axis=1)
        q = q_ref[0, j * BLOCK:(j + 1) * BLOCK, :]
        kwin = kall[j * BLOCK:(j + 3) * BLOCK]
        outs = []
        for kv in range(n_kv):
            g = j * n_kv + kv
            sl = slice(kv * HEAD_DIM, (kv + 1) * HEAD_DIM)
            oa = (_dot(vall[kv, :, j * BLOCK:(j + 3) * BLOCK], pl_ref[g])
                  + _dot(vx_ref[0, kv], px_ref[g]))
            l = oa[HEAD_DIM:HEAD_DIM + 1, :] + es_ref[g]
            outs += _unstack_heads(_rows_to_tokens(oa / l), BLOCK)
            q4 = _stack_heads(q, kv)
            s_lat = _dot_nt(kwin[:, sl], q4) + bias
            s_ctx = _dot_nt(kx[:, sl], q4)
            sink = _sink_row(sink_ref, kv, BLOCK)
            m = jnp.maximum(jnp.maximum(s_lat.max(axis=0, keepdims=True), s_ctx.max(axis=0, keepdims=True)),
                            sink)
            pl_ref[g] = jnp.exp2(s_lat - m).astype(BF16)
            px_ref[g] = jnp.exp2(s_ctx - m).astype(BF16)
            es_ref[g] = jnp.exp2(sink - m)
        blocks.append(jnp.concatenate(outs, axis=1).astype(BF16))
    o_ref[0] = blocks[0] if qb == 1 else jnp.concatenate(blocks, axis=0)


def _window_attention(sink, q, k, vt, kx, vtx, *, qb):
    bn, ln, aw = q.shape
    kw = k.shape[2]
    cn = kx.shape[1]
    n_kv = kw // HEAD_DIM
    nb = ln // BLOCK
    assert kw == LANES and nb % qb == 0
    ni = nb // qb
    n_tiles = bn * ni
    n_grp = qb * n_kv
    mq = Q_PER_KV * BLOCK

    def qk_tile(t):
        t = jnp.minimum(t, n_tiles - 1)
        return t // ni, t % ni

    def pv_tile(t):
        t = jnp.maximum(t - 1, 0)
        return t // ni, t % ni

    def blocks_of(tile):
        cur = lambda t: tile(t)[1]
        prev = lambda t: jnp.maximum(tile(t)[1] * qb - 1, 0)
        nxt = lambda t: jnp.minimum((tile(t)[1] + 1) * qb, nb - 1)
        return cur, prev, nxt

    k_cur, k_prev, k_nxt = blocks_of(qk_tile)
    v_cur, v_prev, v_nxt = blocks_of(pv_tile)
    kspec = lambda rows, f: pl.BlockSpec((1, rows, kw), lambda t: (qk_tile(t)[0], f(t), 0))
    vspec = lambda cols, f: pl.BlockSpec((1, n_kv, VT_ROWS, cols), lambda t: (pv_tile(t)[0], 0, 0, f(t)))
    zero = lambda t: 0
    return pl.pallas_call(
        functools.partial(_win_kernel, qb=qb, ni=ni),
        out_shape=jax.ShapeDtypeStruct((bn, ln, aw), BF16),
        grid=(n_tiles + 1,),
        in_specs=[
            pl.BlockSpec(memory_space=pltpu.SMEM),
            pl.BlockSpec((1, qb * BLOCK, aw), lambda t: (qk_tile(t)[0], k_cur(t), 0)),
            kspec(BLOCK, k_prev), kspec(qb * BLOCK, k_cur), kspec(BLOCK, k_nxt),
            vspec(BLOCK, v_prev), vspec(qb * BLOCK, v_cur), vspec(BLOCK, v_nxt),
            kspec(cn, zero), vspec(cn, zero),
        ],
        out_specs=pl.BlockSpec((1, qb * BLOCK, aw), lambda t: (pv_tile(t)[0], v_cur(t), 0)),
        scratch_shapes=[
            pltpu.VMEM((n_grp, 3 * BLOCK, mq), BF16),
            pltpu.VMEM((n_grp, cn, mq), BF16),
            pltpu.VMEM((n_grp, 1, mq), F32),
        ],
        compiler_params=_cparams(1),
        name="window_attention",
    )(sink, q, k, k, k, vt, vt, vt, kx, vtx)


def _glob_kernel(bound_ref, q_ref, k_ref, v_ref, kx_ref, vx_ref, o_ref, m_ref, p_ref, *, kc):
    s = pl.program_id(0)
    ln = k_ref.shape[2]
    cn = kx_ref.shape[2]
    q = q_ref[0]
    tq = q.shape[0]
    q4 = _stack_heads(q, 0)
    pieces = [(k_ref, v_ref, c0, min(kc, ln - c0), c0) for c0 in range(0, ln, kc)] + [(kx_ref, vx_ref, 0, cn, ln)]

    @pl.when(s == 0)
    def _():
        p_ref[...] = jnp.zeros_like(p_ref)

    bound = bound_ref[0]
    m_ref[...] = jnp.full(m_ref.shape, bound - BOUND_SHIFT, F32)

    @pl.when(bound > BOUND_MAX)
    def _():
        m = None
        for kr, _, c0, n, _ in pieces:
            mc = _dot_nt(kr[0, 0, c0:c0 + n, :], q4).max(axis=0, keepdims=True)
            m = mc if m is None else jnp.maximum(m, mc)
        m_ref[...] = m

    m = m_ref[...]
    acc = None
    for kr, vr, c0, n, off in pieces:
        oc = _dot(vr[0, 0, :, c0:c0 + n], p_ref[off:off + n, :])
        acc = oc if acc is None else acc + oc
        p_ref[off:off + n, :] = jnp.exp2(_dot_nt(kr[0, 0, c0:c0 + n, :], q4) - m).astype(BF16)
    l = acc[HEAD_DIM:HEAD_DIM + 1, :]
    l = jnp.where(l > 0.0, l, 1.0)
    o = _rows_to_tokens(acc / l)
    o_ref[0] = jnp.concatenate(_unstack_heads(o, tq), axis=1).astype(BF16)


def _score_bound(qg, kg):
    return (HEAD_DIM * Q_SCALE * BOUND_SLACK * jnp.max(jnp.abs(qg)) * jnp.max(jnp.abs(kg))).reshape(1)


def _global_attention(bound, q, k, v, kx, vx, *, tq, kc):
    bn, ln, aw = q.shape
    n_kv = k.shape[1]
    cn = kx.shape[2]
    gw = aw // n_kv
    assert gw == Q_PER_KV * HEAD_DIM
    nq = ln // tq
    n_tiles = bn * n_kv * nq

    def split(t):
        return t // (n_kv * nq), (t // nq) % n_kv, t % nq

    def qk_tile(s):
        return split(jnp.minimum(s, n_tiles - 1))

    def pv_tile(s):
        return split(jnp.maximum(s - 1, 0))

    def q_map(s):
        b, h, i = qk_tile(s)
        return b, i, h

    def k_map(s):
        b, h, _ = qk_tile(s)
        return b, h, 0, 0

    def v_map(s):
        b, h, _ = pv_tile(s)
        return b, h, 0, 0

    def o_map(s):
        b, h, i = pv_tile(s)
        return b, i, h

    return pl.pallas_call(
        functools.partial(_glob_kernel, kc=kc),
        out_shape=jax.ShapeDtypeStruct((bn, ln, aw), BF16),
        grid=(n_tiles + 1,),
        in_specs=[
            pl.BlockSpec(memory_space=pltpu.SMEM),
            pl.BlockSpec((1, tq, gw), q_map),
            pl.BlockSpec((1, 1, ln, HEAD_DIM), k_map),
            pl.BlockSpec((1, 1, VT_ROWS, ln), v_map),
            pl.BlockSpec((1, 1, cn, HEAD_DIM), k_map),
            pl.BlockSpec((1, 1, VT_ROWS, cn), v_map),
        ],
        out_specs=pl.BlockSpec((1, tq, gw), o_map),
        scratch_shapes=[pltpu.VMEM((1, Q_PER_KV * tq), F32), pltpu.VMEM((ln + cn, Q_PER_KV * tq), BF16)],
        compiler_params=_cparams(1),
        name="global_attention",
    )(bound, q, k, v, kx, vx)


def _ctx_attn_kernel(sink_ref, wq_ref, wk_ref, wv_ref, gq_ref, gk_ref, gv_ref, ow_ref, og_ref):
    wq = wq_ref[0]
    gq = gq_ref[0]
    wk = wk_ref[0]
    cn, kw = wk.shape
    outs_w = []
    outs_g = []
    for kv in range(kw // HEAD_DIM):
        st = _dot_nt(wk[:, kv * HEAD_DIM:(kv + 1) * HEAD_DIM], _stack_heads(wq, kv))
        sink = _sink_row(sink_ref, kv, cn)
        m = jnp.maximum(st.max(axis=0, keepdims=True), sink)
        oa = _dot(wv_ref[0, kv], jnp.exp2(st - m).astype(BF16))
        l = oa[HEAD_DIM:HEAD_DIM + 1, :] + jnp.exp2(sink - m)
        outs_w += _unstack_heads(_rows_to_tokens(oa / l), cn)
        st = _dot_nt(gk_ref[0, kv], _stack_heads(gq, kv))
        m = st.max(axis=0, keepdims=True)
        oa = _dot(gv_ref[0, kv], jnp.exp2(st - m).astype(BF16))
        outs_g += _unstack_heads(_rows_to_tokens(oa / oa[HEAD_DIM:HEAD_DIM + 1, :]), cn)
    ow_ref[0] = jnp.concatenate(outs_w, axis=1).astype(BF16)
    og_ref[0] = jnp.concatenate(outs_g, axis=1).astype(BF16)


def _context_attention(sink, wq, wk, wv, gq, gk, gv):
    bn, cn, aw = wq.shape
    kw = wk.shape[2]
    n_kv = gk.shape[1]
    tok = lambda w: pl.BlockSpec((1, cn, w), lambda b: (b, 0, 0))
    hd = lambda r, w: pl.BlockSpec((1, n_kv, r, w), lambda b: (b, 0, 0, 0))
    shp = jax.ShapeDtypeStruct((bn, cn, aw), BF16)
    return pl.pallas_call(
        _ctx_attn_kernel,
        out_shape=(shp, shp),
        grid=(bn,),
        in_specs=[pl.BlockSpec(memory_space=pltpu.SMEM), tok(aw), tok(kw), hd(VT_ROWS, cn), tok(aw),
                  hd(cn, HEAD_DIM), hd(VT_ROWS, cn)],
        out_specs=(tok(aw), tok(aw)),
        compiler_params=_cparams(1),
        name="context_attention",
    )(sink, wq, wk, wv, gq, gk, gv)


def _merge_kernel(x_ref, mod_ref, g1_ref, g2_ref, yp_ref, yw_ref, yg_ref, wg_ref, bg_ref, wb_ref, wo_ref,
                  x1_ref, h2_ref):
    x = x_ref[0]
    h = _rms_mod(x, g1_ref[...], mod_ref[0, 0:1, :], mod_ref[0, 1:2, :]).astype(BF16)
    merged = None
    for i, y_ref in enumerate((yp_ref, yw_ref, yg_ref)):
        gate = jax.nn.sigmoid(_dot(h, wg_ref[i]) + bg_ref[i:i + 1, :])
        term = gate * _dot(y_ref[0], wb_ref[i])
        merged = term if merged is None else merged + term
    x1 = x + mod_ref[0, 2:3, :] * _dot(merged.astype(BF16), wo_ref[...])
    x1_ref[0] = x1
    h2_ref[0] = _rms_mod(x1, g2_ref[...], mod_ref[0, 3:4, :], mod_ref[0, 4:5, :]).astype(BF16)


def _merge(x, mod, g1, g2, yp, yw, yg, wg, bg, wb, wo, *, tl):
    bn, ln, d = x.shape
    bw = yp.shape[2]
    mod_map = (lambda b, i: (b, 0, 0)) if mod.shape[0] > 1 else (lambda b, i: (0, 0, 0))
    tok = lambda w: pl.BlockSpec((1, tl, w), lambda b, i: (b, i, 0))
    full = _resident
    return pl.pallas_call(
        _merge_kernel,
        out_shape=(jax.ShapeDtypeStruct((bn, ln, d), F32), jax.ShapeDtypeStruct((bn, ln, d), BF16)),
        grid=(bn, ln // tl),
        in_specs=[tok(d), pl.BlockSpec((1, 6, d), mod_map), full(g1), full(g2), tok(bw), tok(bw), tok(bw),
                  full(wg), full(bg), full(wb), full(wo)],
        out_specs=(tok(d), tok(d)),
        compiler_params=_cparams(2),
        name="merge_branches",
    )(x, mod, g1, g2, yp, yw, yg, wg, bg, wb, wo)


def _ffn_kernel(h_ref, hp_ref, hn_ref, x_ref, mod_ref, wg_ref, wv_ref, cw_ref, cb_ref, wd_ref, fg_ref,
                o_ref, g_scr, *, final):
    j = pl.program_id(1)
    nj = pl.num_programs(1)
    tl = h_ref.shape[1]
    halo = hp_ref.shape[1]
    hm = h_ref[0]
    hp = jnp.where(j > 0, hp_ref[0], jnp.zeros_like(hp_ref[0]))
    hn = jnp.where(j < nj - 1, hn_ref[0], jnp.zeros_like(hn_ref[0]))
    hext = jnp.concatenate([hp, hm, hn], axis=0)
    g_scr[...] = _dot(hext, wg_ref[...])
    a = (g_scr[pl.ds(halo - 1, tl), :] * cw_ref[0:1, :]
         + g_scr[pl.ds(halo, tl), :] * cw_ref[1:2, :]
         + g_scr[pl.ds(halo + 1, tl), :] * cw_ref[2:3, :]
         + cb_ref[...])
    val = _dot(hm, wv_ref[...])
    act = (a * jax.nn.sigmoid(a)) * val
    x2 = x_ref[0] + mod_ref[0, 5:6, :] * _dot(act.astype(BF16), wd_ref[...])
    if final:
        x2 = x2 * lax.rsqrt(jnp.mean(x2 * x2, axis=-1, keepdims=True) + EPS) * fg_ref[...]
    o_ref[0] = x2


def _ffn(h2, x1, mod, wg, wv, cw, cb, wd, fg, *, tl, final):
    bn, ln, d = x1.shape
    f = wg.shape[1]
    halo = BF16_SUBLANES
    r = tl // halo
    nh = ln // halo
    mod_map = (lambda b, i: (b, 0, 0)) if mod.shape[0] > 1 else (lambda b, i: (0, 0, 0))
    tok = pl.BlockSpec((1, tl, d), lambda b, i: (b, i, 0))
    full = _resident
    return pl.pallas_call(
        functools.partial(_ffn_kernel, final=final),
        out_shape=jax.ShapeDtypeStruct((bn, ln, d), F32),
        grid=(bn, ln // tl),
        in_specs=[
            tok,
            pl.BlockSpec((1, halo, d), lambda b, i: (b, jnp.maximum(i * r - 1, 0), 0)),
            pl.BlockSpec((1, halo, d), lambda b, i: (b, jnp.minimum((i + 1) * r, nh - 1), 0)),
            tok,
            pl.BlockSpec((1, 6, d), mod_map),
            full(wg), full(wv), full(cw), full(cb), full(wd), full(fg),
        ],
        out_specs=tok,
        scratch_shapes=[pltpu.VMEM((tl + 2 * halo, f), F32)],
        compiler_params=_cparams(2),
        name="conv_glu_final" if final else "conv_glu",
    )(h2, h2, h2, x1, mod, wg, wv, cw, cb, wd, fg)


def _rope_tables(n_tok):
    pos = jnp.arange(n_tok)
    row = (pos // GRID_W).astype(F32)
    col = (pos % GRID_W).astype(F32)
    half = HEAD_DIM // 2
    inv = ROPE_THETA ** (-jnp.arange(0, half, 2, dtype=F32) / half)
    ang_r = row[:, None] * inv
    ang_c = col[:, None] * inv
    cos64 = jnp.concatenate([jnp.cos(ang_r)] * 2 + [jnp.cos(ang_c)] * 2, axis=1)
    sin64 = jnp.concatenate([-jnp.sin(ang_r), jnp.sin(ang_r), -jnp.sin(ang_c), jnp.sin(ang_c)], axis=1)
    reps = LANES // HEAD_DIM
    return jnp.tile(cos64, (1, reps)), jnp.tile(sin64, (1, reps))


def _tile_rows(n, target):
    t = min(n, target)
    while n % t:
        t //= 2
    return t


def kernel(x, c, ctx, c_ctx, w_mod, b_mod, norm1_g, norm2_g, w_in, w_pool_grp, pool_scale, win_sink,
           q_norm_g, k_norm_g, w_branch, w_gate, b_gate, w_out, w_ff_gate, w_ff_val, conv_w, conv_b,
           w_ff_down, final_g):
    bn, ln, d = x.shape
    cn = ctx.shape[1]
    depth = w_mod.shape[0]
    attn_w = d // 2
    kv_w = attn_w // Q_PER_KV
    n_heads = attn_w // HEAD_DIM

    mod_all = _modulation(jnp.concatenate([c, c_ctx[None]], axis=0), w_mod, b_mod)
    mod_all = mod_all.reshape(depth, bn + 1, 6, d)

    cos, sin = _rope_tables(ln)
    cos_c = jnp.ones((cn, LANES), F32)
    sin_c = jnp.zeros((cn, LANES), F32)
    ids = np.arange(2 * LANES) // HEAD_DIM
    bd = jnp.asarray(ids[:, None] == ids[None, :], dtype=BF16)
    fg = final_g.reshape(1, d)

    tl = _tile_rows(ln, 1024)
    tl_ffn = _tile_rows(ln, 512)
    tl_c = _tile_rows(cn, 256)
    tq = _tile_rows(ln, 512)
    win_qb = _tile_rows(ln // BLOCK, 4)

    for l in range(depth):
        last = l == depth - 1
        mod_l = mod_all[l, :bn]
        mod_c = mod_all[l, bn:]
        g1 = norm1_g[l].reshape(1, d)
        g2 = norm2_g[l].reshape(1, d)
        w_in_l = w_in[l].astype(BF16)
        qg = jnp.tile(q_norm_g[l], attn_w // HEAD_DIM).reshape(1, attn_w)
        kg = jnp.tile(k_norm_g[l], kv_w // HEAD_DIM).reshape(1, kv_w)
        w_grp = w_pool_grp[l].astype(BF16)
        p_scale = pool_scale[l].reshape(1, -1)
        sink = win_sink[l]
        wg = w_gate[l].astype(BF16)
        bg = b_gate[l]
        wb = w_branch[l].astype(BF16)
        wo = w_out[l].astype(BF16)
        wfg = w_ff_gate[l].astype(BF16)
        wfv = w_ff_val[l].astype(BF16)
        wfd = w_ff_down[l].astype(BF16)
        cw = conv_w[l]
        cb = conv_b[l].reshape(1, -1)

        u_l, wq_l, wk_l, wv_l, gq_l, gk_l, gv_l = _inproj(
            x, mod_l, g1, w_in_l, cos, sin, qg, kg, bd, rope=True, tl=tl)
        u_c, wq_c, wk_c, wv_c, gq_c, gk_c, gv_c = _inproj(
            ctx, mod_c, g1, w_in_l, cos_c, sin_c, qg, kg, bd, rope=False, tl=tl_c)

        y_pool = _pool(u_l, w_grp, p_scale)
        y_win = _window_attention(sink, wq_l, wk_l, wv_l, wk_c, wv_c, qb=win_qb)
        bound = _score_bound(q_norm_g[l], k_norm_g[l])
        y_glob = _global_attention(bound, gq_l, gk_l, gv_l, gk_c, gv_c, tq=tq, kc=512)
        x1, h2 = _merge(x, mod_l, g1, g2, y_pool, y_win, y_glob, wg, bg, wb, wo, tl=tl)

        if not last:
            yc_pool = _pool(u_c, w_grp, p_scale)
            yc_win, yc_glob = _context_attention(sink, wq_c, wk_c, wv_c, gq_c, gk_c, gv_c)
            c1, hc2 = _merge(ctx, mod_c, g1, g2, yc_pool, yc_win, yc_glob, wg, bg, wb, wo, tl=tl_c)
            ctx = _ffn(hc2, c1, mod_c, wfg, wfv, cw, cb, wfd, fg, tl=tl_c, final=False)

        x = _ffn(h2, x1, mod_l, wfg, wfv, cw, cb, wfd, fg, tl=tl_ffn, final=last)
    return x
```

```python
import functools

import jax
import jax.numpy as jnp
import numpy as np
from jax import lax
from jax.experimental import pallas as pl
from jax.experimental.pallas import tpu as pltpu

F32 = jnp.float32
BF16 = jnp.bfloat16

GRID_W = 64
HEAD_DIM = 64
Q_PER_KV = 4
POOL_WINDOWS = (2, 4, 8, 16)
POOL_REACH = max(POOL_WINDOWS) // 2
WINDOW = 128
BLOCK = 128
ROPE_THETA = 10000.0
EPS = 1e-6
NEG = -1e30
SM_SCALE = HEAD_DIM ** -0.5
LOG2E = 1.4426950408889634
Q_SCALE = SM_SCALE * LOG2E
BOUND_SLACK = 1.02
BOUND_SHIFT = 60.0
BOUND_MAX = 80.0
LANES = 128
VT_ROWS = 80
BF16_SUBLANES = 16
VMEM_LIMIT = 56 * 1024 * 1024


def _cparams(n_axes):
    return pltpu.CompilerParams(
        dimension_semantics=("arbitrary",) * n_axes,
        vmem_limit_bytes=VMEM_LIMIT,
    )


def _resident(a):
    return pl.BlockSpec(a.shape, lambda *_: (0,) * a.ndim, pipeline_mode=pl.Buffered(1))


def _rms_mod(x, g, shift, scale):
    y = x * lax.rsqrt(jnp.mean(x * x, axis=-1, keepdims=True) + EPS)
    return (y * g) * (1.0 + scale) + shift


def _dot(a, b):
    return jnp.dot(a, b, preferred_element_type=F32)


def _dot_nt(a, b):
    return lax.dot_general(a, b, (((1,), (1,)), ((), ())), preferred_element_type=F32)


def _mod_kernel(c_ref, w_ref, b_ref, o_ref):
    c = c_ref[...]
    s = c * jax.nn.sigmoid(c)
    o_ref[0] = _dot(s.astype(BF16), w_ref[0].astype(BF16)) + b_ref[0]


def _modulation(cc, w_mod, b_mod):
    depth, d, n = w_mod.shape
    rows = cc.shape[0]
    tn = 1536
    return pl.pallas_call(
        _mod_kernel,
        out_shape=jax.ShapeDtypeStruct((depth, rows, n), F32),
        grid=(depth, n // tn),
        in_specs=[
            pl.BlockSpec((rows, d), lambda l, j: (0, 0)),
            pl.BlockSpec((1, d, tn), lambda l, j: (l, 0, j)),
            pl.BlockSpec((1, 1, tn), lambda l, j: (l, 0, j)),
        ],
        out_specs=pl.BlockSpec((1, rows, tn), lambda l, j: (l, 0, j)),
        compiler_params=_cparams(2),
        name="modulation",
    )(cc, w_mod, b_mod.reshape(depth, 1, n))


def _head_rms(t, bd, g):
    sq = t * t
    hi = sq.astype(BF16)
    lo = (sq - hi.astype(F32)).astype(BF16)
    w = t.shape[1]
    parts = []
    for j in range(0, w, 2 * LANES):
        e = min(j + 2 * LANES, w)
        b = bd[: e - j, : e - j]
        parts.append(_dot(hi[:, j:e], b) + _dot(lo[:, j:e], b))
    ssq = parts[0] if len(parts) == 1 else jnp.concatenate(parts, axis=1)
    return t * lax.rsqrt(ssq * (1.0 / HEAD_DIM) + EPS) * g


def _rope(t, cos, sin, partner_lane):
    outs = []
    for j in range(0, t.shape[1], LANES):
        tj = t[:, j:j + LANES]
        partner = jnp.take_along_axis(tj, partner_lane, axis=1)
        outs.append(tj * cos + partner * sin)
    return outs[0] if len(outs) == 1 else jnp.concatenate(outs, axis=1)


def _inproj_kernel(x_ref, mod_ref, g1_ref, w_ref, cos_ref, sin_ref, qg_ref, kg_ref, bd_ref,
                   h_ref, u_ref, wq_ref, wk_ref, wv_ref, gq_ref, gk_ref, gv_ref, *, rope, pool_w, attn_w, kv_w):
    x = x_ref[0]
    h = _rms_mod(x, g1_ref[...], mod_ref[0, 0:1, :], mod_ref[0, 1:2, :])
    h = h.astype(BF16)
    h_ref[0] = h
    mix_w = attn_w + 2 * kv_w
    zg = _dot(h, w_ref[:, pool_w + mix_w:pool_w + 2 * mix_w])
    zw = _dot(h, w_ref[:, pool_w:pool_w + mix_w])
    u_ref[0] = _dot(h, w_ref[:, :pool_w])
    gq, gk, gv = zg[:, :attn_w], zg[:, attn_w:attn_w + kv_w], zg[:, attn_w + kv_w:]
    wq, wk, wv = zw[:, :attn_w], zw[:, attn_w:attn_w + kv_w], zw[:, attn_w + kv_w:]
    bd = bd_ref[...]
    gq = _head_rms(gq, bd, qg_ref[...])
    gk = _head_rms(gk, bd, kg_ref[...])
    if rope:
        cos = cos_ref[...]
        sin = sin_ref[...]
        lane = lax.broadcasted_iota(jnp.int32, cos.shape, 1)
        partner_lane = lane ^ 16
        wq = _rope(wq, cos, sin, partner_lane)
        wk = _rope(wk, cos, sin, partner_lane)
        gq = _rope(gq, cos, sin, partner_lane)
        gk = _rope(gk, cos, sin, partner_lane)
    wq_ref[0] = (wq * Q_SCALE).astype(BF16)
    gq_ref[0] = (gq * Q_SCALE).astype(BF16)
    wk_ref[0] = wk.astype(BF16)
    tail_row = lax.broadcasted_iota(jnp.int32, (VT_ROWS - HEAD_DIM, gv.shape[0]), 0)
    tail = jnp.where(tail_row == 0, 1.0, 0.0).astype(BF16)
    for v, v_ref in ((wv, wv_ref), (gv, gv_ref)):
        vt = v.T
        for k in range(kv_w // HEAD_DIM):
            v_ref[0, k, 0:HEAD_DIM, :] = vt[k * HEAD_DIM:(k + 1) * HEAD_DIM].astype(BF16)
            v_ref[0, k, HEAD_DIM:VT_ROWS, :] = tail
    for k in range(kv_w // HEAD_DIM):
        gk_ref[0, k] = gk[:, k * HEAD_DIM:(k + 1) * HEAD_DIM].astype(BF16)


def _inproj(x, mod, g1, w_in, cos, sin, qg, kg, bd, *, rope, tl):
    bn, ln, d = x.shape
    in_w = w_in.shape[1]
    attn_w = qg.shape[1]
    kv_w = kg.shape[1]
    pool_w = in_w - 2 * attn_w - 4 * kv_w
    n_kv = kv_w // HEAD_DIM
    mod_b = mod.shape[0]
    mod_map = (lambda b, i: (b, 0, 0)) if mod_b > 1 else (lambda b, i: (0, 0, 0))
    tok = lambda w: pl.BlockSpec((1, tl, w), lambda b, i: (b, i, 0))
    full2 = _resident
    kern = functools.partial(_inproj_kernel, rope=rope, pool_w=pool_w, attn_w=attn_w, kv_w=kv_w)
    out_shapes = (
        jax.ShapeDtypeStruct((bn, ln, d), BF16),
        jax.ShapeDtypeStruct((bn, ln, pool_w), F32),
        jax.ShapeDtypeStruct((bn, ln, attn_w), BF16),
        jax.ShapeDtypeStruct((bn, ln, kv_w), BF16),
        jax.ShapeDtypeStruct((bn, n_kv, VT_ROWS, ln), BF16),
        jax.ShapeDtypeStruct((bn, ln, attn_w), BF16),
        jax.ShapeDtypeStruct((bn, n_kv, ln, HEAD_DIM), BF16),
        jax.ShapeDtypeStruct((bn, n_kv, VT_ROWS, ln), BF16),
    )
    assert kv_w == LANES
    head_spec = pl.BlockSpec((1, n_kv, tl, HEAD_DIM), lambda b, i: (b, 0, i, 0))
    vt_spec = pl.BlockSpec((1, n_kv, VT_ROWS, tl), lambda b, i: (b, 0, 0, i))
    return pl.pallas_call(
        kern,
        out_shape=out_shapes,
        grid=(bn, ln // tl),
        in_specs=[
            tok(d),
            pl.BlockSpec((1, 6, d), mod_map),
            full2(g1),
            full2(w_in),
            pl.BlockSpec((tl, LANES), lambda b, i: (i, 0)),
            pl.BlockSpec((tl, LANES), lambda b, i: (i, 0)),
            full2(qg),
            full2(kg),
            full2(bd),
        ],
        out_specs=(tok(d), tok(pool_w), tok(attn_w), tok(kv_w), vt_spec, tok(attn_w), head_spec, vt_spec),
        compiler_params=_cparams(2),
        name="inproj_rope" if rope else "inproj_ctx",
    )(x, mod, g1, w_in, cos, sin, qg, kg, bd)


def _pool_tile(u_ref, up_ref, un_ref, w_ref, s_ref, ln):
    i = pl.program_id(1)
    ni = pl.num_programs(1)
    tl = u_ref.shape[1]
    n_grp = len(POOL_WINDOWS)
    gw = u_ref.shape[2] // n_grp
    ext_len = tl + 2 * POOL_REACH
    up = jnp.where(i > 0, up_ref[0], 0.0)
    un = jnp.where(i < ni - 1, un_ref[0], 0.0)
    t = i * tl + lax.broadcasted_iota(jnp.int32, (tl, 1), 0)
    outs = []
    for gi, win in enumerate(POOL_WINDOWS):
        sl = slice(gi * gw, (gi + 1) * gw)
        u = u_ref[0, :, sl]
        e = jnp.concatenate([up[:, sl], u, un[:, sl]], axis=0)
        acc = e + pltpu.roll(e, 1, axis=0)
        half = 1
        while 2 * half < win:
            acc = pltpu.roll(acc, half, axis=0) + pltpu.roll(acc, ext_len - half, axis=0)
            half *= 2
        wsum = acc[POOL_REACH:POOL_REACH + tl, :]
        cnt = (jnp.minimum(t + win // 2, ln) - jnp.maximum(t - win // 2, 0)).astype(F32)
        p = wsum / cnt - u
        outs.append((_dot(p.astype(BF16), w_ref[gi]) * s_ref[:, sl]).astype(BF16))
    return jnp.concatenate(outs, axis=1)


def _stack_heads(q, kv):
    return jnp.concatenate(
        [q[:, (kv * Q_PER_KV + g) * HEAD_DIM:(kv * Q_PER_KV + g + 1) * HEAD_DIM] for g in range(Q_PER_KV)], axis=0)


def _rows_to_tokens(ot):
    pad = LANES - ot.shape[0]
    if pad:
        ot = jnp.concatenate([ot, jnp.zeros((pad, ot.shape[1]), F32)], axis=0)
    return ot.T[:, :HEAD_DIM]


def _unstack_heads(o, t):
    return [o[g * t:(g + 1) * t] for g in range(Q_PER_KV)]


def _sink_row(sink_ref, kv, t):
    lane = lax.broadcasted_iota(jnp.int32, (1, Q_PER_KV * t), 1)
    row = jnp.full((1, Q_PER_KV * t), sink_ref[kv * Q_PER_KV + Q_PER_KV - 1] * LOG2E, F32)
    for g in range(Q_PER_KV - 2, -1, -1):
        row = jnp.where(lane < (g + 1) * t, sink_ref[kv * Q_PER_KV + g] * LOG2E, row)
    return row


def _win_kernel(sink_ref, q_ref, kp_ref, kc_ref, kn_ref, vp_ref, vc_ref, vn_ref, kx_ref, vx_ref, o_ref,
                pl_ref, px_ref, es_ref, *, qb, ni):
    t = pl.program_id(0)
    i = jnp.minimum(t, pl.num_programs(0) - 2) % ni
    kall = jnp.concatenate([kp_ref[0], kc_ref[0], kn_ref[0]], axis=0)
    vall = jnp.concatenate([vp_ref[0], vc_ref[0], vn_ref[0]], axis=2)
    kx = kx_ref[0]
    n_kv = kall.shape[1] // HEAD_DIM
    key = lax.broadcasted_iota(jnp.int32, (3 * BLOCK, BLOCK), 0)
    qry = lax.broadcasted_iota(jnp.int32, (3 * BLOCK, BLOCK), 1)
    band = jnp.where(jnp.abs(key - BLOCK - qry) <= WINDOW, 0.0, NEG)
    no_prev = jnp.where((key < BLOCK) & (i == 0), NEG, 0.0)
    no_next = jnp.where((key >= 2 * BLOCK) & (i == ni - 1), NEG, 0.0)

    @pl.when(t == 0)
    def _():
        pl_ref[...] = jnp.zeros_like(pl_ref)
        px_ref[...] = jnp.zeros_like(px_ref)
        es_ref[...] = jnp.ones_like(es_ref)

    blocks = []
    for j in range(qb):
        bias = band
        if j == 0:
            bias = bias + no_prev
        if j == qb - 1:
            bias = bias + no_next
        bias = jnp.concatenate([bias] * Q_PER_KV, axis=1)
        q = q_ref[0, j * BLOCK:(j + 1) * BLOCK, :]
        kwin = kall[j * BLOCK:(j + 3) * BLOCK]
        outs = []
        for kv in range(n_kv):
            g = j * n_kv + kv
            sl = slice(kv * HEAD_DIM, (kv + 1) * HEAD_DIM)
            oa = (_dot(vall[kv, :, j * BLOCK:(j + 3) * BLOCK], pl_ref[g])
                  + _dot(vx_ref[0, kv], px_ref[g]))
            l = oa[HEAD_DIM:HEAD_DIM + 1, :] + es_ref[g]
            outs += _unstack_heads(_rows_to_tokens(oa / l), BLOCK)
            q4 = _stack_heads(q, kv)
            s_lat = _dot_nt(kwin[:, sl], q4) + bias
            s_ctx = _dot_nt(kx[:, sl], q4)
            sink = _sink_row(sink_ref, kv, BLOCK)
            m = jnp.maximum(jnp.maximum(s_lat.max(axis=0, keepdims=True), s_ctx.max(axis=0, keepdims=True)),
                            sink)
            pl_ref[g] = jnp.exp2(s_lat - m).astype(BF16)
            px_ref[g] = jnp.exp2(s_ctx - m).astype(BF16)
            es_ref[g] = jnp.exp2(sink - m)
        blocks.append(jnp.concatenate(outs, axis=1).astype(BF16))
    o_ref[0] = blocks[0] if qb == 1 else jnp.concatenate(blocks, axis=0)


def _window_attention(sink, q, k, vt, kx, vtx, *, qb):
    bn, ln, aw = q.shape
    kw = k.shape[2]
    cn = kx.shape[1]
    n_kv = kw // HEAD_DIM
    nb = ln // BLOCK
    assert kw == LANES and nb % qb == 0
    ni = nb // qb
    n_tiles = bn * ni
    n_grp = qb * n_kv
    mq = Q_PER_KV * BLOCK

    def qk_tile(t):
        t = jnp.minimum(t, n_tiles - 1)
        return t // ni, t % ni

    def pv_tile(t):
        t = jnp.maximum(t - 1, 0)
        return t // ni, t % ni

    def blocks_of(tile):
        cur = lambda t: tile(t)[1]
        prev = lambda t: jnp.maximum(tile(t)[1] * qb - 1, 0)
        nxt = lambda t: jnp.minimum((tile(t)[1] + 1) * qb, nb - 1)
        return cur, prev, nxt

    k_cur, k_prev, k_nxt = blocks_of(qk_tile)
    v_cur, v_prev, v_nxt = blocks_of(pv_tile)
    kspec = lambda rows, f: pl.BlockSpec((1, rows, kw), lambda t: (qk_tile(t)[0], f(t), 0))
    vspec = lambda cols, f: pl.BlockSpec((1, n_kv, VT_ROWS, cols), lambda t: (pv_tile(t)[0], 0, 0, f(t)))
    zero = lambda t: 0
    return pl.pallas_call(
        functools.partial(_win_kernel, qb=qb, ni=ni),
        out_shape=jax.ShapeDtypeStruct((bn, ln, aw), BF16),
        grid=(n_tiles + 1,),
        in_specs=[
            pl.BlockSpec(memory_space=pltpu.SMEM),
            pl.BlockSpec((1, qb * BLOCK, aw), lambda t: (qk_tile(t)[0], k_cur(t), 0)),
            kspec(BLOCK, k_prev), kspec(qb * BLOCK, k_cur), kspec(BLOCK, k_nxt),
            vspec(BLOCK, v_prev), vspec(qb * BLOCK, v_cur), vspec(BLOCK, v_nxt),
            kspec(cn, zero), vspec(cn, zero),
        ],
        out_specs=pl.BlockSpec((1, qb * BLOCK, aw), lambda t: (pv_tile(t)[0], v_cur(t), 0)),
        scratch_shapes=[
            pltpu.VMEM((n_grp, 3 * BLOCK, mq), BF16),
            pltpu.VMEM((n_grp, cn, mq), BF16),
            pltpu.VMEM((n_grp, 1, mq), F32),
        ],
        compiler_params=_cparams(1),
        name="window_attention",
    )(sink, q, k, k, k, vt, vt, vt, kx, vtx)


def _glob_kernel(bound_ref, q_ref, k_ref, v_ref, kx_ref, vx_ref, o_ref, m_ref, p_ref, *, kc):
    s = pl.program_id(0)
    ln = k_ref.shape[2]
    cn = kx_ref.shape[2]
    q = q_ref[0]
    tq = q.shape[0]
    q4 = _stack_heads(q, 0)
    pieces = [(k_ref, v_ref, c0, min(kc, ln - c0), c0) for c0 in range(0, ln, kc)] + [(kx_ref, vx_ref, 0, cn, ln)]

    @pl.when(s == 0)
    def _():
        p_ref[...] = jnp.zeros_like(p_ref)

    bound = bound_ref[0]
    m_ref[...] = jnp.full(m_ref.shape, bound - BOUND_SHIFT, F32)

    @pl.when(bound > BOUND_MAX)
    def _():
        m = None
        for kr, _, c0, n, _ in pieces:
            mc = _dot_nt(kr[0, 0, c0:c0 + n, :], q4).max(axis=0, keepdims=True)
            m = mc if m is None else jnp.maximum(m, mc)
        m_ref[...] = m

    m = m_ref[...]
    acc = None
    for kr, vr, c0, n, off in pieces:
        oc = _dot(vr[0, 0, :, c0:c0 + n], p_ref[off:off + n, :])
        acc = oc if acc is None else acc + oc
        p_ref[off:off + n, :] = jnp.exp2(_dot_nt(kr[0, 0, c0:c0 + n, :], q4) - m).astype(BF16)
    l = acc[HEAD_DIM:HEAD_DIM + 1, :]
    l = jnp.where(l > 0.0, l, 1.0)
    o = _rows_to_tokens(acc / l)
    o_ref[0] = jnp.concatenate(_unstack_heads(o, tq), axis=1).astype(BF16)


def _score_bound(qg, kg):
    return (HEAD_DIM * Q_SCALE * BOUND_SLACK * jnp.max(jnp.abs(qg)) * jnp.max(jnp.abs(kg))).reshape(1)


def _global_attention(bound, q, k, v, kx, vx, *, tq, kc):
    bn, ln, aw = q.shape
    n_kv = k.shape[1]
    cn = kx.shape[2]
    gw = aw // n_kv
    assert gw == Q_PER_KV * HEAD_DIM
    nq = ln // tq
    n_tiles = bn * n_kv * nq

    def split(t):
        return t // (n_kv * nq), (t // nq) % n_kv, t % nq

    def qk_tile(s):
        return split(jnp.minimum(s, n_tiles - 1))

    def pv_tile(s):
        return split(jnp.maximum(s - 1, 0))

    def q_map(s):
        b, h, i = qk_tile(s)
        return b, i, h

    def k_map(s):
        b, h, _ = qk_tile(s)
        return b, h, 0, 0

    def v_map(s):
        b, h, _ = pv_tile(s)
        return b, h, 0, 0

    def o_map(s):
        b, h, i = pv_tile(s)
        return b, i, h

    return pl.pallas_call(
        functools.partial(_glob_kernel, kc=kc),
        out_shape=jax.ShapeDtypeStruct((bn, ln, aw), BF16),
        grid=(n_tiles + 1,),
        in_specs=[
            pl.BlockSpec(memory_space=pltpu.SMEM),
            pl.BlockSpec((1, tq, gw), q_map),
            pl.BlockSpec((1, 1, ln, HEAD_DIM), k_map),
            pl.BlockSpec((1, 1, VT_ROWS, ln), v_map),
            pl.BlockSpec((1, 1, cn, HEAD_DIM), k_map),
            pl.BlockSpec((1, 1, VT_ROWS, cn), v_map),
        ],
        out_specs=pl.BlockSpec((1, tq, gw), o_map),
        scratch_shapes=[pltpu.VMEM((1, Q_PER_KV * tq), F32), pltpu.VMEM((ln + cn, Q_PER_KV * tq), BF16)],
        compiler_params=_cparams(1),
        name="global_attention",
    )(bound, q, k, v, kx, vx)


def _ctx_attn_kernel(sink_ref, wq_ref, wk_ref, wv_ref, gq_ref, gk_ref, gv_ref, ow_ref, og_ref):
    wq = wq_ref[0]
    gq = gq_ref[0]
    wk = wk_ref[0]
    cn, kw = wk.shape
    outs_w = []
    outs_g = []
    for kv in range(kw // HEAD_DIM):
        st = _dot_nt(wk[:, kv * HEAD_DIM:(kv + 1) * HEAD_DIM], _stack_heads(wq, kv))
        sink = _sink_row(sink_ref, kv, cn)
        m = jnp.maximum(st.max(axis=0, keepdims=True), sink)
        oa = _dot(wv_ref[0, kv], jnp.exp2(st - m).astype(BF16))
        l = oa[HEAD_DIM:HEAD_DIM + 1, :] + jnp.exp2(sink - m)
        outs_w += _unstack_heads(_rows_to_tokens(oa / l), cn)
        st = _dot_nt(gk_ref[0, kv], _stack_heads(gq, kv))
        m = st.max(axis=0, keepdims=True)
        oa = _dot(gv_ref[0, kv], jnp.exp2(st - m).astype(BF16))
        outs_g += _unstack_heads(_rows_to_tokens(oa / oa[HEAD_DIM:HEAD_DIM + 1, :]), cn)
    ow_ref[0] = jnp.concatenate(outs_w, axis=1).astype(BF16)
    og_ref[0] = jnp.concatenate(outs_g, axis=1).astype(BF16)


def _context_attention(sink, wq, wk, wv, gq, gk, gv):
    bn, cn, aw = wq.shape
    kw = wk.shape[2]
    n_kv = gk.shape[1]
    tok = lambda w: pl.BlockSpec((1, cn, w), lambda b: (b, 0, 0))
    hd = lambda r, w: pl.BlockSpec((1, n_kv, r, w), lambda b: (b, 0, 0, 0))
    shp = jax.ShapeDtypeStruct((bn, cn, aw), BF16)
    return pl.pallas_call(
        _ctx_attn_kernel,
        out_shape=(shp, shp),
        grid=(bn,),
        in_specs=[pl.BlockSpec(memory_space=pltpu.SMEM), tok(aw), tok(kw), hd(VT_ROWS, cn), tok(aw),
                  hd(cn, HEAD_DIM), hd(VT_ROWS, cn)],
        out_specs=(tok(aw), tok(aw)),
        compiler_params=_cparams(1),
        name="context_attention",
    )(sink, wq, wk, wv, gq, gk, gv)


def _merge_kernel(x_ref, h_ref, mod_ref, g2_ref, u_ref, up_ref, un_ref, wp_ref, ps_ref, yw_ref, yg_ref,
                  wg_ref, bg_ref, wb_ref, wo_ref, x1_ref, h2_ref, *, ln):
    x = x_ref[0]
    h = h_ref[0]
    ys = ((1, yw_ref[0]), (2, yg_ref[0]), (0, _pool_tile(u_ref, up_ref, un_ref, wp_ref, ps_ref, ln)))
    merged = None
    for i, y in ys:
        gate = jax.nn.sigmoid(_dot(h, wg_ref[i]) + bg_ref[i:i + 1, :])
        term = gate * _dot(y, wb_ref[i])
        merged = term if merged is None else merged + term
    x1 = x + mod_ref[0, 2:3, :] * _dot(merged.astype(BF16), wo_ref[...])
    x1_ref[0] = x1
    h2_ref[0] = _rms_mod(x1, g2_ref[...], mod_ref[0, 3:4, :], mod_ref[0, 4:5, :]).astype(BF16)


def _merge(x, h, mod, g2, u, w_grp, p_scale, yw, yg, wg, bg, wb, wo, *, tl):
    bn, ln, d = x.shape
    bw = yw.shape[2]
    r = tl // POOL_REACH
    n_halo = ln // POOL_REACH
    halo = lambda f: pl.BlockSpec((1, POOL_REACH, bw), f)
    mod_map = (lambda b, i: (b, 0, 0)) if mod.shape[0] > 1 else (lambda b, i: (0, 0, 0))
    tok = lambda w: pl.BlockSpec((1, tl, w), lambda b, i: (b, i, 0))
    full = _resident
    return pl.pallas_call(
        functools.partial(_merge_kernel, ln=ln),
        out_shape=(jax.ShapeDtypeStruct((bn, ln, d), F32), jax.ShapeDtypeStruct((bn, ln, d), BF16)),
        grid=(bn, ln // tl),
        in_specs=[tok(d), tok(d), pl.BlockSpec((1, 6, d), mod_map), full(g2),
                  tok(bw),
                  halo(lambda b, i: (b, jnp.maximum(i * r - 1, 0), 0)),
                  halo(lambda b, i: (b, jnp.minimum((i + 1) * r, n_halo - 1), 0)),
                  full(w_grp), full(p_scale), tok(bw), tok(bw),
                  full(wg), full(bg), full(wb), full(wo)],
        out_specs=(tok(d), tok(d)),
        compiler_params=_cparams(2),
        name="merge_branches",
    )(x, h, mod, g2, u, u, u, w_grp, p_scale, yw, yg, wg, bg, wb, wo)


def _ffn_kernel(h_ref, hp_ref, hn_ref, x_ref, mod_ref, wg_ref, wv_ref, cw_ref, cb_ref, wd_ref, fg_ref,
                o_ref, g_scr, *, final):
    j = pl.program_id(1)
    nj = pl.num_programs(1)
    tl = h_ref.shape[1]
    halo = hp_ref.shape[1]
    hm = h_ref[0]
    hp = jnp.where(j > 0, hp_ref[0], jnp.zeros_like(hp_ref[0]))
    hn = jnp.where(j < nj - 1, hn_ref[0], jnp.zeros_like(hn_ref[0]))
    hext = jnp.concatenate([hp, hm, hn], axis=0)
    g_scr[...] = _dot(hext, wg_ref[...])
    a = (g_scr[pl.ds(halo - 1, tl), :] * cw_ref[0:1, :]
         + g_scr[pl.ds(halo, tl), :] * cw_ref[1:2, :]
         + g_scr[pl.ds(halo + 1, tl), :] * cw_ref[2:3, :]
         + cb_ref[...])
    val = _dot(hm, wv_ref[...])
    act = (a * jax.nn.sigmoid(a)) * val
    x2 = x_ref[0] + mod_ref[0, 5:6, :] * _dot(act.astype(BF16), wd_ref[...])
    if final:
        x2 = x2 * lax.rsqrt(jnp.mean(x2 * x2, axis=-1, keepdims=True) + EPS) * fg_ref[...]
    o_ref[0] = x2


def _ffn(h2, x1, mod, wg, wv, cw, cb, wd, fg, *, tl, final):
    bn, ln, d = x1.shape
    f = wg.shape[1]
    halo = BF16_SUBLANES
    r = tl // halo
    nh = ln // halo
    mod_map = (lambda b, i: (b, 0, 0)) if mod.shape[0] > 1 else (lambda b, i: (0, 0, 0))
    tok = pl.BlockSpec((1, tl, d), lambda b, i: (b, i, 0))
    full = _resident
    return pl.pallas_call(
        functools.partial(_ffn_kernel, final=final),
        out_shape=jax.ShapeDtypeStruct((bn, ln, d), F32),
        grid=(bn, ln // tl),
        in_specs=[
            tok,
            pl.BlockSpec((1, halo, d), lambda b, i: (b, jnp.maximum(i * r - 1, 0), 0)),
            pl.BlockSpec((1, halo, d), lambda b, i: (b, jnp.minimum((i + 1) * r, nh - 1), 0)),
            tok,
            pl.BlockSpec((1, 6, d), mod_map),
            full(wg), full(wv), full(cw), full(cb), full(wd), full(fg),
        ],
        out_specs=tok,
        scratch_shapes=[pltpu.VMEM((tl + 2 * halo, f), F32)],
        compiler_params=_cparams(2),
        name="conv_glu_final" if final else "conv_glu",
    )(h2, h2, h2, x1, mod, wg, wv, cw, cb, wd, fg)


def _rope_tables(n_tok):
    pos = jnp.arange(n_tok)
    row = (pos // GRID_W).astype(F32)
    col = (pos % GRID_W).astype(F32)
    half = HEAD_DIM // 2
    inv = ROPE_THETA ** (-jnp.arange(0, half, 2, dtype=F32) / half)
    ang_r = row[:, None] * inv
    ang_c = col[:, None] * inv
    cos64 = jnp.concatenate([jnp.cos(ang_r)] * 2 + [jnp.cos(ang_c)] * 2, axis=1)
    sin64 = jnp.concatenate([-jnp.sin(ang_r), jnp.sin(ang_r), -jnp.sin(ang_c), jnp.sin(ang_c)], axis=1)
    reps = LANES // HEAD_DIM
    return jnp.tile(cos64, (1, reps)), jnp.tile(sin64, (1, reps))


def _tile_rows(n, target):
    t = min(n, target)
    while n % t:
        t //= 2
    return t


def kernel(x, c, ctx, c_ctx, w_mod, b_mod, norm1_g, norm2_g, w_in, w_pool_grp, pool_scale, win_sink,
           q_norm_g, k_norm_g, w_branch, w_gate, b_gate, w_out, w_ff_gate, w_ff_val, conv_w, conv_b,
           w_ff_down, final_g):
    bn, ln, d = x.shape
    cn = ctx.shape[1]
    depth = w_mod.shape[0]
    attn_w = d // 2
    kv_w = attn_w // Q_PER_KV
    n_heads = attn_w // HEAD_DIM

    mod_all = _modulation(jnp.concatenate([c, c_ctx[None]], axis=0), w_mod, b_mod)
    mod_all = mod_all.reshape(depth, bn + 1, 6, d)

    cos, sin = _rope_tables(ln)
    cos_c = jnp.ones((cn, LANES), F32)
    sin_c = jnp.zeros((cn, LANES), F32)
    ids = np.arange(2 * LANES) // HEAD_DIM
    bd = jnp.asarray(ids[:, None] == ids[None, :], dtype=BF16)
    fg = final_g.reshape(1, d)

    tl = _tile_rows(ln, 1024)
    tl_ffn = _tile_rows(ln, 512)
    tl_c = _tile_rows(cn, 256)
    tq = _tile_rows(ln, 512)
    win_qb = _tile_rows(ln // BLOCK, 8)

    for l in range(depth):
        last = l == depth - 1
        mod_l = mod_all[l, :bn]
        mod_c = mod_all[l, bn:]
        g1 = norm1_g[l].reshape(1, d)
        g2 = norm2_g[l].reshape(1, d)
        w_in_l = w_in[l].astype(BF16)
        qg = jnp.tile(q_norm_g[l], attn_w // HEAD_DIM).reshape(1, attn_w)
        kg = jnp.tile(k_norm_g[l], kv_w // HEAD_DIM).reshape(1, kv_w)
        w_grp = w_pool_grp[l].astype(BF16)
        p_scale = pool_scale[l].reshape(1, -1)
        sink = win_sink[l]
        wg = w_gate[l].astype(BF16)
        bg = b_gate[l]
        wb = w_branch[l].astype(BF16)
        wo = w_out[l].astype(BF16)
        wfg = w_ff_gate[l].astype(BF16)
        wfv = w_ff_val[l].astype(BF16)
        wfd = w_ff_down[l].astype(BF16)
        cw = conv_w[l]
        cb = conv_b[l].reshape(1, -1)

        h_l, u_l, wq_l, wk_l, wv_l, gq_l, gk_l, gv_l = _inproj(
            x, mod_l, g1, w_in_l, cos, sin, qg, kg, bd, rope=True, tl=tl)
        h_c, u_c, wq_c, wk_c, wv_c, gq_c, gk_c, gv_c = _inproj(
            ctx, mod_c, g1, w_in_l, cos_c, sin_c, qg, kg, bd, rope=False, tl=tl_c)

        y_win = _window_attention(sink, wq_l, wk_l, wv_l, wk_c, wv_c, qb=win_qb)
        bound = _score_bound(q_norm_g[l], k_norm_g[l])
        y_glob = _global_attention(bound, gq_l, gk_l, gv_l, gk_c, gv_c, tq=tq, kc=1024)
        x1, h2 = _merge(x, h_l, mod_l, g2, u_l, w_grp, p_scale, y_win, y_glob, wg, bg, wb, wo, tl=tl)

        if not last:
            yc_win, yc_glob = _context_attention(sink, wq_c, wk_c, wv_c, gq_c, gk_c, gv_c)
            c1, hc2 = _merge(ctx, h_c, mod_c, g2, u_c, w_grp, p_scale, yc_win, yc_glob, wg, bg, wb, wo, tl=tl_c)
            ctx = _ffn(hc2, c1, mod_c, wfg, wfv, cw, cb, wfd, fg, tl=tl_c, final=False)

        x = _ffn(h2, x1, mod_l, wfg, wfv, cw, cb, wfd, fg, tl=tl_ffn, final=last)
    return x
```

```python
import functools

import jax
import jax.numpy as jnp
import numpy as np
from jax import lax
from jax.experimental import pallas as pl
from jax.experimental.pallas import tpu as pltpu

F32 = jnp.float32
BF16 = jnp.bfloat16

GRID_W = 64
HEAD_DIM = 64
Q_PER_KV = 4
POOL_WINDOWS = (2, 4, 8, 16)
POOL_REACH = max(POOL_WINDOWS) // 2
WINDOW = 128
BLOCK = 128
ROPE_THETA = 10000.0
EPS = 1e-6
NEG = -1e30
SM_SCALE = HEAD_DIM ** -0.5
LOG2E = 1.4426950408889634
Q_SCALE = SM_SCALE * LOG2E
BOUND_SLACK = 1.02
BOUND_SHIFT = 60.0
BOUND_MAX = 80.0
LANES = 128
VT_ROWS = 80
BF16_SUBLANES = 16
VMEM_LIMIT = 56 * 1024 * 1024

DENSE_TILE = 1024
FFN_TILE = 512
CTX_TILE = 256
GLOBAL_QUERY_TILE = 512
GLOBAL_KEY_CHUNK = 1024
WINDOW_BLOCKS_PER_STEP = 8
MOD_COLS = 1536


def _cparams(n_axes):
    return pltpu.CompilerParams(
        dimension_semantics=("arbitrary",) * n_axes,
        vmem_limit_bytes=VMEM_LIMIT,
    )


def _resident(a):
    return pl.BlockSpec(a.shape, lambda *_: (0,) * a.ndim, pipeline_mode=pl.Buffered(1))


def _rms_mod(x, g, shift, scale):
    y = x * lax.rsqrt(jnp.mean(x * x, axis=-1, keepdims=True) + EPS)
    return (y * g) * (1.0 + scale) + shift


def _dot(a, b):
    return jnp.dot(a, b, preferred_element_type=F32)


def _dot_nt(a, b):
    return lax.dot_general(a, b, (((1,), (1,)), ((), ())), preferred_element_type=F32)


def _mod_kernel(c_ref, w_ref, b_ref, o_ref):
    c = c_ref[...]
    s = c * jax.nn.sigmoid(c)
    o_ref[0] = _dot(s.astype(BF16), w_ref[0].astype(BF16)) + b_ref[0]


def _modulation(cc, w_mod, b_mod):
    depth, d, n = w_mod.shape
    rows = cc.shape[0]
    tn = _tile_rows(n, MOD_COLS)
    return pl.pallas_call(
        _mod_kernel,
        out_shape=jax.ShapeDtypeStruct((depth, rows, n), F32),
        grid=(depth, n // tn),
        in_specs=[
            pl.BlockSpec((rows, d), lambda l, j: (0, 0)),
            pl.BlockSpec((1, d, tn), lambda l, j: (l, 0, j)),
            pl.BlockSpec((1, 1, tn), lambda l, j: (l, 0, j)),
        ],
        out_specs=pl.BlockSpec((1, rows, tn), lambda l, j: (l, 0, j)),
        compiler_params=_cparams(2),
        name="modulation",
    )(cc, w_mod, b_mod.reshape(depth, 1, n))


def _head_rms(t, bd, g):
    sq = t * t
    hi = sq.astype(BF16)
    lo = (sq - hi.astype(F32)).astype(BF16)
    w = t.shape[1]
    parts = []
    for j in range(0, w, 2 * LANES):
        e = min(j + 2 * LANES, w)
        b = bd[: e - j, : e - j]
        parts.append(_dot(hi[:, j:e], b) + _dot(lo[:, j:e], b))
    ssq = parts[0] if len(parts) == 1 else jnp.concatenate(parts, axis=1)
    return t * lax.rsqrt(ssq * (1.0 / HEAD_DIM) + EPS) * g


def _rope(t, cos, sin, partner_lane):
    outs = []
    for j in range(0, t.shape[1], LANES):
        tj = t[:, j:j + LANES]
        partner = jnp.take_along_axis(tj, partner_lane, axis=1)
        outs.append(tj * cos + partner * sin)
    return outs[0] if len(outs) == 1 else jnp.concatenate(outs, axis=1)


def _inproj_kernel(x_ref, mod_ref, g1_ref, w_ref, cos_ref, sin_ref, qg_ref, kg_ref, bd_ref,
                   h_ref, u_ref, wq_ref, wk_ref, wv_ref, gq_ref, gk_ref, gv_ref, *, rope, pool_w, attn_w, kv_w):
    x = x_ref[0]
    h = _rms_mod(x, g1_ref[...], mod_ref[0, 0:1, :], mod_ref[0, 1:2, :])
    h = h.astype(BF16)
    h_ref[0] = h
    mix_w = attn_w + 2 * kv_w
    zg = _dot(h, w_ref[:, pool_w + mix_w:pool_w + 2 * mix_w])
    zw = _dot(h, w_ref[:, pool_w:pool_w + mix_w])
    u_ref[0] = _dot(h, w_ref[:, :pool_w])
    gq, gk, gv = zg[:, :attn_w], zg[:, attn_w:attn_w + kv_w], zg[:, attn_w + kv_w:]
    wq, wk, wv = zw[:, :attn_w], zw[:, attn_w:attn_w + kv_w], zw[:, attn_w + kv_w:]
    bd = bd_ref[...]
    gq = _head_rms(gq, bd, qg_ref[...])
    gk = _head_rms(gk, bd, kg_ref[...])
    if rope:
        cos = cos_ref[...]
        sin = sin_ref[...]
        lane = lax.broadcasted_iota(jnp.int32, cos.shape, 1)
        partner_lane = lane ^ 16
        wq = _rope(wq, cos, sin, partner_lane)
        wk = _rope(wk, cos, sin, partner_lane)
        gq = _rope(gq, cos, sin, partner_lane)
        gk = _rope(gk, cos, sin, partner_lane)
    wq_ref[0] = (wq * Q_SCALE).astype(BF16)
    gq_ref[0] = (gq * Q_SCALE).astype(BF16)
    wk_ref[0] = wk.astype(BF16)
    tail_row = lax.broadcasted_iota(jnp.int32, (VT_ROWS - HEAD_DIM, gv.shape[0]), 0)
    tail = jnp.where(tail_row == 0, 1.0, 0.0).astype(BF16)
    for v, v_ref in ((wv, wv_ref), (gv, gv_ref)):
        vt = v.T
        for k in range(kv_w // HEAD_DIM):
            v_ref[0, k, 0:HEAD_DIM, :] = vt[k * HEAD_DIM:(k + 1) * HEAD_DIM].astype(BF16)
            v_ref[0, k, HEAD_DIM:VT_ROWS, :] = tail
    for k in range(kv_w // HEAD_DIM):
        gk_ref[0, k] = gk[:, k * HEAD_DIM:(k + 1) * HEAD_DIM].astype(BF16)


def _inproj(x, mod, g1, w_in, cos, sin, qg, kg, bd, *, rope, tl):
    bn, ln, d = x.shape
    in_w = w_in.shape[1]
    attn_w = qg.shape[1]
    kv_w = kg.shape[1]
    pool_w = in_w - 2 * attn_w - 4 * kv_w
    n_kv = kv_w // HEAD_DIM
    mod_b = mod.shape[0]
    mod_map = (lambda b, i: (b, 0, 0)) if mod_b > 1 else (lambda b, i: (0, 0, 0))
    tok = lambda w: pl.BlockSpec((1, tl, w), lambda b, i: (b, i, 0))
    full2 = _resident
    kern = functools.partial(_inproj_kernel, rope=rope, pool_w=pool_w, attn_w=attn_w, kv_w=kv_w)
    out_shapes = (
        jax.ShapeDtypeStruct((bn, ln, d), BF16),
        jax.ShapeDtypeStruct((bn, ln, pool_w), F32),
        jax.ShapeDtypeStruct((bn, ln, attn_w), BF16),
        jax.ShapeDtypeStruct((bn, ln, kv_w), BF16),
        jax.ShapeDtypeStruct((bn, n_kv, VT_ROWS, ln), BF16),
        jax.ShapeDtypeStruct((bn, ln, attn_w), BF16),
        jax.ShapeDtypeStruct((bn, n_kv, ln, HEAD_DIM), BF16),
        jax.ShapeDtypeStruct((bn, n_kv, VT_ROWS, ln), BF16),
    )
    assert kv_w == LANES
    head_spec = pl.BlockSpec((1, n_kv, tl, HEAD_DIM), lambda b, i: (b, 0, i, 0))
    vt_spec = pl.BlockSpec((1, n_kv, VT_ROWS, tl), lambda b, i: (b, 0, 0, i))
    return pl.pallas_call(
        kern,
        out_shape=out_shapes,
        grid=(bn, ln // tl),
        in_specs=[
            tok(d),
            pl.BlockSpec((1, 6, d), mod_map),
            full2(g1),
            full2(w_in),
            pl.BlockSpec((tl, LANES), lambda b, i: (i, 0)),
            pl.BlockSpec((tl, LANES), lambda b, i: (i, 0)),
            full2(qg),
            full2(kg),
            full2(bd),
        ],
        out_specs=(tok(d), tok(pool_w), tok(attn_w), tok(kv_w), vt_spec, tok(attn_w), head_spec, vt_spec),
        compiler_params=_cparams(2),
        name="inproj_rope" if rope else "inproj_ctx",
    )(x, mod, g1, w_in, cos, sin, qg, kg, bd)


def _pool_tile(u_ref, up_ref, un_ref, w_ref, s_ref, ln):
    i = pl.program_id(1)
    ni = pl.num_programs(1)
    tl = u_ref.shape[1]
    n_grp = len(POOL_WINDOWS)
    gw = u_ref.shape[2] // n_grp
    ext_len = tl + 2 * POOL_REACH
    up = jnp.where(i > 0, up_ref[0], 0.0)
    un = jnp.where(i < ni - 1, un_ref[0], 0.0)
    t = i * tl + lax.broadcasted_iota(jnp.int32, (tl, 1), 0)
    outs = []
    for gi, win in enumerate(POOL_WINDOWS):
        sl = slice(gi * gw, (gi + 1) * gw)
        u = u_ref[0, :, sl]
        e = jnp.concatenate([up[:, sl], u, un[:, sl]], axis=0)
        half = win // 2
        fwd = e
        step = 1
        while step < half:
            fwd = fwd + pltpu.roll(fwd, ext_len - step, axis=0)
            step *= 2
        wsum = (fwd + pltpu.roll(fwd, half, axis=0))[POOL_REACH:POOL_REACH + tl, :]
        cnt = (jnp.minimum(t + win // 2, ln) - jnp.maximum(t - win // 2, 0)).astype(F32)
        p = wsum / cnt - u
        outs.append((_dot(p.astype(BF16), w_ref[gi]) * s_ref[:, sl]).astype(BF16))
    return jnp.concatenate(outs, axis=1)


def _stack_heads(q, kv):
    return jnp.concatenate(
        [q[:, (kv * Q_PER_KV + g) * HEAD_DIM:(kv * Q_PER_KV + g + 1) * HEAD_DIM] for g in range(Q_PER_KV)], axis=0)


def _rows_to_tokens(ot):
    pad = LANES - ot.shape[0]
    if pad:
        ot = jnp.concatenate([ot, jnp.zeros((pad, ot.shape[1]), F32)], axis=0)
    return ot.T[:, :HEAD_DIM]


def _unstack_heads(o, t):
    return [o[g * t:(g + 1) * t] for g in range(Q_PER_KV)]


def _sink_row(sink_ref, kv, t):
    lane = lax.broadcasted_iota(jnp.int32, (1, Q_PER_KV * t), 1)
    row = jnp.full((1, Q_PER_KV * t), sink_ref[kv * Q_PER_KV + Q_PER_KV - 1] * LOG2E, F32)
    for g in range(Q_PER_KV - 2, -1, -1):
        row = jnp.where(lane < (g + 1) * t, sink_ref[kv * Q_PER_KV + g] * LOG2E, row)
    return row


def _win_kernel(sink_ref, q_ref, kp_ref, kc_ref, kn_ref, vp_ref, vc_ref, vn_ref, kx_ref, vx_ref, o_ref,
                pl_ref, px_ref, es_ref, *, qb, ni):
    t = pl.program_id(0)
    i = jnp.minimum(t, pl.num_programs(0) - 2) % ni
    kall = jnp.concatenate([kp_ref[0], kc_ref[0], kn_ref[0]], axis=0)
    vall = jnp.concatenate([vp_ref[0], vc_ref[0], vn_ref[0]], axis=2)
    kx = kx_ref[0]
    n_kv = kall.shape[1] // HEAD_DIM
    key = lax.broadcasted_iota(jnp.int32, (3 * BLOCK, BLOCK), 0)
    qry = lax.broadcasted_iota(jnp.int32, (3 * BLOCK, BLOCK), 1)
    band = jnp.where(jnp.abs(key - BLOCK - qry) <= WINDOW, 0.0, NEG)
    no_prev = jnp.where((key < BLOCK) & (i == 0), NEG, 0.0)
    no_next = jnp.where((key >= 2 * BLOCK) & (i == ni - 1), NEG, 0.0)

    @pl.when(t == 0)
    def _():
        pl_ref[...] = jnp.zeros_like(pl_ref)
        px_ref[...] = jnp.zeros_like(px_ref)
        es_ref[...] = jnp.ones_like(es_ref)

    blocks = []
    for j in range(qb):
        bias = band
        if j == 0:
            bias = bias + no_prev
        if j == qb - 1:
            bias = bias + no_next
        bias = jnp.concatenate([bias] * Q_PER_KV, axis=1)
        q = q_ref[0, j * BLOCK:(j + 1) * BLOCK, :]
        kwin = kall[j * BLOCK:(j + 3) * BLOCK]
        outs = []
        for kv in range(n_kv):
            g = j * n_kv + kv
            sl = slice(kv * HEAD_DIM, (kv + 1) * HEAD_DIM)
            oa = (_dot(vall[kv, :, j * BLOCK:(j + 3) * BLOCK], pl_ref[g])
                  + _dot(vx_ref[0, kv], px_ref[g]))
            l = oa[HEAD_DIM:HEAD_DIM + 1, :] + es_ref[g]
            outs += _unstack_heads(_rows_to_tokens(oa / l), BLOCK)
            q4 = _stack_heads(q, kv)
            s_lat = _dot_nt(kwin[:, sl], q4) + bias
            s_ctx = _dot_nt(kx[:, sl], q4)
            sink = _sink_row(sink_ref, kv, BLOCK)
            m = jnp.maximum(jnp.maximum(s_lat.max(axis=0, keepdims=True), s_ctx.max(axis=0, keepdims=True)),
                            sink)
            pl_ref[g] = jnp.exp2(s_lat - m).astype(BF16)
            px_ref[g] = jnp.exp2(s_ctx - m).astype(BF16)
            es_ref[g] = jnp.exp2(sink - m)
        blocks.append(jnp.concatenate(outs, axis=1).astype(BF16))
    o_ref[0] = blocks[0] if qb == 1 else jnp.concatenate(blocks, axis=0)


def _window_attention(sink, q, k, vt, kx, vtx, *, qb):
    bn, ln, aw = q.shape
    kw = k.shape[2]
    cn = kx.shape[1]
    n_kv = kw // HEAD_DIM
    nb = ln // BLOCK
    assert kw == LANES and nb % qb == 0
    ni = nb // qb
    n_tiles = bn * ni
    n_grp = qb * n_kv
    mq = Q_PER_KV * BLOCK

    def qk_tile(t):
        t = jnp.minimum(t, n_tiles - 1)
        return t // ni, t % ni

    def pv_tile(t):
        t = jnp.maximum(t - 1, 0)
        return t // ni, t % ni

    def blocks_of(tile):
        cur = lambda t: tile(t)[1]
        prev = lambda t: jnp.maximum(tile(t)[1] * qb - 1, 0)
        nxt = lambda t: jnp.minimum((tile(t)[1] + 1) * qb, nb - 1)
        return cur, prev, nxt

    k_cur, k_prev, k_nxt = blocks_of(qk_tile)
    v_cur, v_prev, v_nxt = blocks_of(pv_tile)
    kspec = lambda rows, f: pl.BlockSpec((1, rows, kw), lambda t: (qk_tile(t)[0], f(t), 0))
    vspec = lambda cols, f: pl.BlockSpec((1, n_kv, VT_ROWS, cols), lambda t: (pv_tile(t)[0], 0, 0, f(t)))
    zero = lambda t: 0
    return pl.pallas_call(
        functools.partial(_win_kernel, qb=qb, ni=ni),
        out_shape=jax.ShapeDtypeStruct((bn, ln, aw), BF16),
        grid=(n_tiles + 1,),
        in_specs=[
            pl.BlockSpec(memory_space=pltpu.SMEM),
            pl.BlockSpec((1, qb * BLOCK, aw), lambda t: (qk_tile(t)[0], k_cur(t), 0)),
            kspec(BLOCK, k_prev), kspec(qb * BLOCK, k_cur), kspec(BLOCK, k_nxt),
            vspec(BLOCK, v_prev), vspec(qb * BLOCK, v_cur), vspec(BLOCK, v_nxt),
            kspec(cn, zero), vspec(cn, zero),
        ],
        out_specs=pl.BlockSpec((1, qb * BLOCK, aw), lambda t: (pv_tile(t)[0], v_cur(t), 0)),
        scratch_shapes=[
            pltpu.VMEM((n_grp, 3 * BLOCK, mq), BF16),
            pltpu.VMEM((n_grp, cn, mq), BF16),
            pltpu.VMEM((n_grp, 1, mq), F32),
        ],
        compiler_params=_cparams(1),
        name="window_attention",
    )(sink, q, k, k, k, vt, vt, vt, kx, vtx)


def _glob_kernel(bound_ref, q_ref, k_ref, v_ref, kx_ref, vx_ref, o_ref, m_ref, p_ref, *, kc):
    s = pl.program_id(0)
    ln = k_ref.shape[2]
    cn = kx_ref.shape[2]
    q = q_ref[0]
    tq = q.shape[0]
    q4 = _stack_heads(q, 0)
    pieces = [(k_ref, v_ref, c0, min(kc, ln - c0), c0) for c0 in range(0, ln, kc)] + [(kx_ref, vx_ref, 0, cn, ln)]

    @pl.when(s == 0)
    def _():
        p_ref[...] = jnp.zeros_like(p_ref)

    bound = bound_ref[0]
    m_ref[...] = jnp.full(m_ref.shape, bound - BOUND_SHIFT, F32)

    @pl.when(bound > BOUND_MAX)
    def _():
        m = None
        for kr, _, c0, n, _ in pieces:
            mc = _dot_nt(kr[0, 0, c0:c0 + n, :], q4).max(axis=0, keepdims=True)
            m = mc if m is None else jnp.maximum(m, mc)
        m_ref[...] = m

    m = m_ref[...]
    acc = None
    for kr, vr, c0, n, off in pieces:
        oc = _dot(vr[0, 0, :, c0:c0 + n], p_ref[off:off + n, :])
        acc = oc if acc is None else acc + oc
        p_ref[off:off + n, :] = jnp.exp2(_dot_nt(kr[0, 0, c0:c0 + n, :], q4) - m).astype(BF16)
    l = acc[HEAD_DIM:HEAD_DIM + 1, :]
    l = jnp.where(l > 0.0, l, 1.0)
    o = _rows_to_tokens(acc / l)
    o_ref[0] = jnp.concatenate(_unstack_heads(o, tq), axis=1).astype(BF16)


def _score_bound(qg, kg):
    return (HEAD_DIM * Q_SCALE * BOUND_SLACK * jnp.max(jnp.abs(qg)) * jnp.max(jnp.abs(kg))).reshape(1)


def _global_attention(bound, q, k, v, kx, vx, *, tq, kc):
    bn, ln, aw = q.shape
    n_kv = k.shape[1]
    cn = kx.shape[2]
    gw = aw // n_kv
    assert gw == Q_PER_KV * HEAD_DIM
    nq = ln // tq
    n_tiles = bn * n_kv * nq

    def split(t):
        return t // (n_kv * nq), (t // nq) % n_kv, t % nq

    def qk_tile(s):
        return split(jnp.minimum(s, n_tiles - 1))

    def pv_tile(s):
        return split(jnp.maximum(s - 1, 0))

    def q_map(s):
        b, h, i = qk_tile(s)
        return b, i, h

    def k_map(s):
        b, h, _ = qk_tile(s)
        return b, h, 0, 0

    def v_map(s):
        b, h, _ = pv_tile(s)
        return b, h, 0, 0

    def o_map(s):
        b, h, i = pv_tile(s)
        return b, i, h

    return pl.pallas_call(
        functools.partial(_glob_kernel, kc=kc),
        out_shape=jax.ShapeDtypeStruct((bn, ln, aw), BF16),
        grid=(n_tiles + 1,),
        in_specs=[
            pl.BlockSpec(memory_space=pltpu.SMEM),
            pl.BlockSpec((1, tq, gw), q_map),
            pl.BlockSpec((1, 1, ln, HEAD_DIM), k_map),
            pl.BlockSpec((1, 1, VT_ROWS, ln), v_map),
            pl.BlockSpec((1, 1, cn, HEAD_DIM), k_map),
            pl.BlockSpec((1, 1, VT_ROWS, cn), v_map),
        ],
        out_specs=pl.BlockSpec((1, tq, gw), o_map),
        scratch_shapes=[pltpu.VMEM((1, Q_PER_KV * tq), F32), pltpu.VMEM((ln + cn, Q_PER_KV * tq), BF16)],
        compiler_params=_cparams(1),
        name="global_attention",
    )(bound, q, k, v, kx, vx)


def _ctx_attn_kernel(sink_ref, wq_ref, wk_ref, wv_ref, gq_ref, gk_ref, gv_ref, ow_ref, og_ref):
    wq = wq_ref[0]
    gq = gq_ref[0]
    wk = wk_ref[0]
    cn, kw = wk.shape
    outs_w = []
    outs_g = []
    for kv in range(kw // HEAD_DIM):
        st = _dot_nt(wk[:, kv * HEAD_DIM:(kv + 1) * HEAD_DIM], _stack_heads(wq, kv))
        sink = _sink_row(sink_ref, kv, cn)
        m = jnp.maximum(st.max(axis=0, keepdims=True), sink)
        oa = _dot(wv_ref[0, kv], jnp.exp2(st - m).astype(BF16))
        l = oa[HEAD_DIM:HEAD_DIM + 1, :] + jnp.exp2(sink - m)
        outs_w += _unstack_heads(_rows_to_tokens(oa / l), cn)
        st = _dot_nt(gk_ref[0, kv], _stack_heads(gq, kv))
        m = st.max(axis=0, keepdims=True)
        oa = _dot(gv_ref[0, kv], jnp.exp2(st - m).astype(BF16))
        outs_g += _unstack_heads(_rows_to_tokens(oa / oa[HEAD_DIM:HEAD_DIM + 1, :]), cn)
    ow_ref[0] = jnp.concatenate(outs_w, axis=1).astype(BF16)
    og_ref[0] = jnp.concatenate(outs_g, axis=1).astype(BF16)


def _context_attention(sink, wq, wk, wv, gq, gk, gv):
    bn, cn, aw = wq.shape
    kw = wk.shape[2]
    n_kv = gk.shape[1]
    tok = lambda w: pl.BlockSpec((1, cn, w), lambda b: (b, 0, 0))
    hd = lambda r, w: pl.BlockSpec((1, n_kv, r, w), lambda b: (b, 0, 0, 0))
    shp = jax.ShapeDtypeStruct((bn, cn, aw), BF16)
    return pl.pallas_call(
        _ctx_attn_kernel,
        out_shape=(shp, shp),
        grid=(bn,),
        in_specs=[pl.BlockSpec(memory_space=pltpu.SMEM), tok(aw), tok(kw), hd(VT_ROWS, cn), tok(aw),
                  hd(cn, HEAD_DIM), hd(VT_ROWS, cn)],
        out_specs=(tok(aw), tok(aw)),
        compiler_params=_cparams(1),
        name="context_attention",
    )(sink, wq, wk, wv, gq, gk, gv)


def _merge_kernel(x_ref, h_ref, mod_ref, g2_ref, u_ref, up_ref, un_ref, wp_ref, ps_ref, yw_ref, yg_ref,
                  wg_ref, bg_ref, wb_ref, wo_ref, x1_ref, h2_ref, *, ln):
    x = x_ref[0]
    h = h_ref[0]
    ys = ((1, yw_ref[0]), (2, yg_ref[0]), (0, _pool_tile(u_ref, up_ref, un_ref, wp_ref, ps_ref, ln)))
    merged = None
    for i, y in ys:
        gate = jax.nn.sigmoid(_dot(h, wg_ref[i]) + bg_ref[i:i + 1, :])
        term = gate * _dot(y, wb_ref[i])
        merged = term if merged is None else merged + term
    x1 = x + mod_ref[0, 2:3, :] * _dot(merged.astype(BF16), wo_ref[...])
    x1_ref[0] = x1
    h2_ref[0] = _rms_mod(x1, g2_ref[...], mod_ref[0, 3:4, :], mod_ref[0, 4:5, :]).astype(BF16)


def _merge(x, h, mod, g2, u, w_grp, p_scale, yw, yg, wg, bg, wb, wo, *, tl):
    bn, ln, d = x.shape
    bw = yw.shape[2]
    r = tl // POOL_REACH
    n_halo = ln // POOL_REACH
    halo = lambda f: pl.BlockSpec((1, POOL_REACH, bw), f)
    mod_map = (lambda b, i: (b, 0, 0)) if mod.shape[0] > 1 else (lambda b, i: (0, 0, 0))
    tok = lambda w: pl.BlockSpec((1, tl, w), lambda b, i: (b, i, 0))
    full = _resident
    return pl.pallas_call(
        functools.partial(_merge_kernel, ln=ln),
        out_shape=(jax.ShapeDtypeStruct((bn, ln, d), F32), jax.ShapeDtypeStruct((bn, ln, d), BF16)),
        grid=(bn, ln // tl),
        in_specs=[tok(d), tok(d), pl.BlockSpec((1, 6, d), mod_map), full(g2),
                  tok(bw),
                  halo(lambda b, i: (b, jnp.maximum(i * r - 1, 0), 0)),
                  halo(lambda b, i: (b, jnp.minimum((i + 1) * r, n_halo - 1), 0)),
                  full(w_grp), full(p_scale), tok(bw), tok(bw),
                  full(wg), full(bg), full(wb), full(wo)],
        out_specs=(tok(d), tok(d)),
        compiler_params=_cparams(2),
        name="merge_branches",
    )(x, h, mod, g2, u, u, u, w_grp, p_scale, yw, yg, wg, bg, wb, wo)


def _ffn_kernel(h_ref, hp_ref, hn_ref, x_ref, mod_ref, wg_ref, wv_ref, cw_ref, cb_ref, wd_ref, fg_ref,
                o_ref, g_scr, *, final):
    j = pl.program_id(1)
    nj = pl.num_programs(1)
    tl = h_ref.shape[1]
    halo = hp_ref.shape[1]
    hm = h_ref[0]
    hp = jnp.where(j > 0, hp_ref[0], jnp.zeros_like(hp_ref[0]))
    hn = jnp.where(j < nj - 1, hn_ref[0], jnp.zeros_like(hn_ref[0]))
    hext = jnp.concatenate([hp, hm, hn], axis=0)
    g_scr[...] = _dot(hext, wg_ref[...])
    a = (g_scr[pl.ds(halo - 1, tl), :] * cw_ref[0:1, :]
         + g_scr[pl.ds(halo, tl), :] * cw_ref[1:2, :]
         + g_scr[pl.ds(halo + 1, tl), :] * cw_ref[2:3, :]
         + cb_ref[...])
    val = _dot(hm, wv_ref[...])
    act = (a * jax.nn.sigmoid(a)) * val
    x2 = x_ref[0] + mod_ref[0, 5:6, :] * _dot(act.astype(BF16), wd_ref[...])
    if final:
        x2 = x2 * lax.rsqrt(jnp.mean(x2 * x2, axis=-1, keepdims=True) + EPS) * fg_ref[...]
    o_ref[0] = x2


def _ffn(h2, x1, mod, wg, wv, cw, cb, wd, fg, *, tl, final):
    bn, ln, d = x1.shape
    f = wg.shape[1]
    halo = BF16_SUBLANES
    r = tl // halo
    nh = ln // halo
    mod_map = (lambda b, i: (b, 0, 0)) if mod.shape[0] > 1 else (lambda b, i: (0, 0, 0))
    tok = pl.BlockSpec((1, tl, d), lambda b, i: (b, i, 0))
    full = _resident
    return pl.pallas_call(
        functools.partial(_ffn_kernel, final=final),
        out_shape=jax.ShapeDtypeStruct((bn, ln, d), F32),
        grid=(bn, ln // tl),
        in_specs=[
            tok,
            pl.BlockSpec((1, halo, d), lambda b, i: (b, jnp.maximum(i * r - 1, 0), 0)),
            pl.BlockSpec((1, halo, d), lambda b, i: (b, jnp.minimum((i + 1) * r, nh - 1), 0)),
            tok,
            pl.BlockSpec((1, 6, d), mod_map),
            full(wg), full(wv), full(cw), full(cb), full(wd), full(fg),
        ],
        out_specs=tok,
        scratch_shapes=[pltpu.VMEM((tl + 2 * halo, f), F32)],
        compiler_params=_cparams(2),
        name="conv_glu_final" if final else "conv_glu",
    )(h2, h2, h2, x1, mod, wg, wv, cw, cb, wd, fg)


def _rope_tables(n_tok):
    pos = jnp.arange(n_tok)
    row = (pos // GRID_W).astype(F32)
    col = (pos % GRID_W).astype(F32)
    half = HEAD_DIM // 2
    inv = ROPE_THETA ** (-jnp.arange(0, half, 2, dtype=F32) / half)
    ang_r = row[:, None] * inv
    ang_c = col[:, None] * inv
    cos64 = jnp.concatenate([jnp.cos(ang_r)] * 2 + [jnp.cos(ang_c)] * 2, axis=1)
    sin64 = jnp.concatenate([-jnp.sin(ang_r), jnp.sin(ang_r), -jnp.sin(ang_c), jnp.sin(ang_c)], axis=1)
    reps = LANES // HEAD_DIM
    return jnp.tile(cos64, (1, reps)), jnp.tile(sin64, (1, reps))


def _tile_rows(n, target):
    t = min(n, target)
    while n % t:
        t //= 2
    return t


def kernel(x, c, ctx, c_ctx, w_mod, b_mod, norm1_g, norm2_g, w_in, w_pool_grp, pool_scale, win_sink,
           q_norm_g, k_norm_g, w_branch, w_gate, b_gate, w_out, w_ff_gate, w_ff_val, conv_w, conv_b,
           w_ff_down, final_g):
    bn, ln, d = x.shape
    cn = ctx.shape[1]
    depth = w_mod.shape[0]
    attn_w = d // 2
    kv_w = attn_w // Q_PER_KV

    mod_all = _modulation(jnp.concatenate([c, c_ctx[None]], axis=0), w_mod, b_mod)
    mod_all = mod_all.reshape(depth, bn + 1, 6, d)

    cos, sin = _rope_tables(ln)
    cos_c = jnp.ones((cn, LANES), F32)
    sin_c = jnp.zeros((cn, LANES), F32)
    ids = np.arange(2 * LANES) // HEAD_DIM
    bd = jnp.asarray(ids[:, None] == ids[None, :], dtype=BF16)
    fg = final_g.reshape(1, d)

    tl = _tile_rows(ln, DENSE_TILE)
    tl_ffn = _tile_rows(ln, FFN_TILE)
    tl_c = _tile_rows(cn, CTX_TILE)
    tq = _tile_rows(ln, GLOBAL_QUERY_TILE)
    win_qb = _tile_rows(ln // BLOCK, WINDOW_BLOCKS_PER_STEP)

    for l in range(depth):
        last = l == depth - 1
        mod_l = mod_all[l, :bn]
        mod_c = mod_all[l, bn:]
        g1 = norm1_g[l].reshape(1, d)
        g2 = norm2_g[l].reshape(1, d)
        w_in_l = w_in[l].astype(BF16)
        qg = jnp.tile(q_norm_g[l], attn_w // HEAD_DIM).reshape(1, attn_w)
        kg = jnp.tile(k_norm_g[l], kv_w // HEAD_DIM).reshape(1, kv_w)
        w_grp = w_pool_grp[l].astype(BF16)
        p_scale = pool_scale[l].reshape(1, -1)
        sink = win_sink[l]
        wg = w_gate[l].astype(BF16)
        bg = b_gate[l]
        wb = w_branch[l].astype(BF16)
        wo = w_out[l].astype(BF16)
        wfg = w_ff_gate[l].astype(BF16)
        wfv = w_ff_val[l].astype(BF16)
        wfd = w_ff_down[l].astype(BF16)
        cw = conv_w[l]
        cb = conv_b[l].reshape(1, -1)

        h_l, u_l, wq_l, wk_l, wv_l, gq_l, gk_l, gv_l = _inproj(
            x, mod_l, g1, w_in_l, cos, sin, qg, kg, bd, rope=True, tl=tl)
        h_c, u_c, wq_c, wk_c, wv_c, gq_c, gk_c, gv_c = _inproj(
            ctx, mod_c, g1, w_in_l, cos_c, sin_c, qg, kg, bd, rope=False, tl=tl_c)

        y_win = _window_attention(sink, wq_l, wk_l, wv_l, wk_c, wv_c, qb=win_qb)
        bound = _score_bound(q_norm_g[l], k_norm_g[l])
        y_glob = _global_attention(bound, gq_l, gk_l, gv_l, gk_c, gv_c, tq=tq, kc=GLOBAL_KEY_CHUNK)
        x1, h2 = _merge(x, h_l, mod_l, g2, u_l, w_grp, p_scale, y_win, y_glob, wg, bg, wb, wo, tl=tl)

        if not last:
            yc_win, yc_glob = _context_attention(sink, wq_c, wk_c, wv_c, gq_c, gk_c, gv_c)
            c1, hc2 = _merge(ctx, h_c, mod_c, g2, u_c, w_grp, p_scale, yc_win, yc_glob, wg, bg, wb, wo, tl=tl_c)
            ctx = _ffn(hc2, c1, mod_c, wfg, wfv, cw, cb, wfd, fg, tl=tl_c, final=False)

        x = _ffn(h2, x1, mod_l, wfg, wfv, cw, cb, wfd, fg, tl=tl_ffn, final=last)
    return x
```

```python
import functools

import jax
import jax.numpy as jnp
import numpy as np
from jax import lax
from jax.experimental import pallas as pl
from jax.experimental.pallas import tpu as pltpu

F32 = jnp.float32
BF16 = jnp.bfloat16

GRID_W = 64
HEAD_DIM = 64
Q_PER_KV = 4
POOL_WINDOWS = (2, 4, 8, 16)
POOL_REACH = max(POOL_WINDOWS) // 2
WINDOW = 128
BLOCK = 128
ROPE_THETA = 10000.0
EPS = 1e-6
NEG = -1e30
SM_SCALE = HEAD_DIM ** -0.5
LOG2E = 1.4426950408889634
Q_SCALE = SM_SCALE * LOG2E
BOUND_SLACK = 1.02
BOUND_SHIFT = 60.0
BOUND_MAX = 80.0
LANES = 128
VT_ROWS = 80
BF16_SUBLANES = 16
VMEM_LIMIT = 56 * 1024 * 1024

DENSE_TILE = 1024
FFN_TILE = 512
CTX_TILE = 256
GLOBAL_QUERY_TILE = 512
GLOBAL_KEY_CHUNK = 1024
WINDOW_BLOCKS_PER_STEP = 8
MOD_COLS = 1536


def _cparams(n_axes):
    return pltpu.CompilerParams(
        dimension_semantics=("arbitrary",) * n_axes,
        vmem_limit_bytes=VMEM_LIMIT,
    )


def _resident(a):
    return pl.BlockSpec(a.shape, lambda *_: (0,) * a.ndim, pipeline_mode=pl.Buffered(1))


def _rms_mod(x, g, shift, scale):
    y = x * lax.rsqrt(jnp.mean(x * x, axis=-1, keepdims=True) + EPS)
    return (y * g) * (1.0 + scale) + shift


def _dot(a, b):
    return jnp.dot(a, b, preferred_element_type=F32)


def _dot_nt(a, b):
    return lax.dot_general(a, b, (((1,), (1,)), ((), ())), preferred_element_type=F32)


def _mod_kernel(c_ref, w_ref, b_ref, o_ref):
    c = c_ref[...]
    s = c * jax.nn.sigmoid(c)
    o_ref[0] = _dot(s.astype(BF16), w_ref[0].astype(BF16)) + b_ref[0]


def _modulation(cc, w_mod, b_mod):
    depth, d, n = w_mod.shape
    rows = cc.shape[0]
    tn = _tile_rows(n, MOD_COLS)
    return pl.pallas_call(
        _mod_kernel,
        out_shape=jax.ShapeDtypeStruct((depth, rows, n), F32),
        grid=(depth, n // tn),
        in_specs=[
            pl.BlockSpec((rows, d), lambda l, j: (0, 0)),
            pl.BlockSpec((1, d, tn), lambda l, j: (l, 0, j)),
            pl.BlockSpec((1, 1, tn), lambda l, j: (l, 0, j)),
        ],
        out_specs=pl.BlockSpec((1, rows, tn), lambda l, j: (l, 0, j)),
        compiler_params=_cparams(2),
        name="modulation",
    )(cc, w_mod, b_mod.reshape(depth, 1, n))


def _head_rms(t, bd, g):
    sq = t * t
    hi = sq.astype(BF16)
    lo = (sq - hi.astype(F32)).astype(BF16)
    w = t.shape[1]
    parts = []
    for j in range(0, w, 2 * LANES):
        e = min(j + 2 * LANES, w)
        b = bd[: e - j, : e - j]
        parts.append(_dot(hi[:, j:e], b) + _dot(lo[:, j:e], b))
    ssq = parts[0] if len(parts) == 1 else jnp.concatenate(parts, axis=1)
    return t * lax.rsqrt(ssq * (1.0 / HEAD_DIM) + EPS) * g


def _rope(t, cos, sin, partner_lane):
    outs = []
    for j in range(0, t.shape[1], LANES):
        tj = t[:, j:j + LANES]
        partner = jnp.take_along_axis(tj, partner_lane, axis=1)
        outs.append(tj * cos + partner * sin)
    return outs[0] if len(outs) == 1 else jnp.concatenate(outs, axis=1)


def _inproj_kernel(x_ref, mod_ref, g1_ref, w_ref, cos_ref, sin_ref, qg_ref, kg_ref, bd_ref,
                   h_ref, u_ref, wq_ref, wk_ref, wv_ref, gq_ref, gk_ref, gv_ref, *, rope, pool_w, attn_w, kv_w):
    x = x_ref[0]
    h = _rms_mod(x, g1_ref[...], mod_ref[0, 0:1, :], mod_ref[0, 1:2, :])
    h = h.astype(BF16)
    h_ref[0] = h
    mix_w = attn_w + 2 * kv_w
    zg = _dot(h, w_ref[:, pool_w + mix_w:pool_w + 2 * mix_w])
    zw = _dot(h, w_ref[:, pool_w:pool_w + mix_w])
    u_ref[0] = _dot(h, w_ref[:, :pool_w])
    gq, gk, gv = zg[:, :attn_w], zg[:, attn_w:attn_w + kv_w], zg[:, attn_w + kv_w:]
    wq, wk, wv = zw[:, :attn_w], zw[:, attn_w:attn_w + kv_w], zw[:, attn_w + kv_w:]
    bd = bd_ref[...]
    gq = _head_rms(gq, bd, qg_ref[...])
    gk = _head_rms(gk, bd, kg_ref[...])
    if rope:
        cos = cos_ref[...]
        sin = sin_ref[...]
        lane = lax.broadcasted_iota(jnp.int32, cos.shape, 1)
        partner_lane = lane ^ 16
        wq = _rope(wq, cos, sin, partner_lane)
        wk = _rope(wk, cos, sin, partner_lane)
        gq = _rope(gq, cos, sin, partner_lane)
        gk = _rope(gk, cos, sin, partner_lane)
    wq_ref[0] = (wq * Q_SCALE).astype(BF16)
    gq_ref[0] = (gq * Q_SCALE).astype(BF16)
    wk_ref[0] = wk.astype(BF16)
    tail_row = lax.broadcasted_iota(jnp.int32, (VT_ROWS - HEAD_DIM, gv.shape[0]), 0)
    tail = jnp.where(tail_row == 0, 1.0, 0.0).astype(BF16)
    for v, v_ref in ((wv, wv_ref), (gv, gv_ref)):
        vt = v.T
        for k in range(kv_w // HEAD_DIM):
            v_ref[0, k, 0:HEAD_DIM, :] = vt[k * HEAD_DIM:(k + 1) * HEAD_DIM].astype(BF16)
            v_ref[0, k, HEAD_DIM:VT_ROWS, :] = tail
    for k in range(kv_w // HEAD_DIM):
        gk_ref[0, k] = gk[:, k * HEAD_DIM:(k + 1) * HEAD_DIM].astype(BF16)


def _inproj(x, mod, g1, w_in, cos, sin, qg, kg, bd, *, rope, tl):
    bn, ln, d = x.shape
    in_w = w_in.shape[1]
    attn_w = qg.shape[1]
    kv_w = kg.shape[1]
    pool_w = in_w - 2 * attn_w - 4 * kv_w
    n_kv = kv_w // HEAD_DIM
    mod_b = mod.shape[0]
    mod_map = (lambda b, i: (b, 0, 0)) if mod_b > 1 else (lambda b, i: (0, 0, 0))
    tok = lambda w: pl.BlockSpec((1, tl, w), lambda b, i: (b, i, 0))
    full2 = _resident
    kern = functools.partial(_inproj_kernel, rope=rope, pool_w=pool_w, attn_w=attn_w, kv_w=kv_w)
    out_shapes = (
        jax.ShapeDtypeStruct((bn, ln, d), BF16),
        jax.ShapeDtypeStruct((bn, ln, pool_w), F32),
        jax.ShapeDtypeStruct((bn, ln, attn_w), BF16),
        jax.ShapeDtypeStruct((bn, ln, kv_w), BF16),
        jax.ShapeDtypeStruct((bn, n_kv, VT_ROWS, ln), BF16),
        jax.ShapeDtypeStruct((bn, ln, attn_w), BF16),
        jax.ShapeDtypeStruct((bn, n_kv, ln, HEAD_DIM), BF16),
        jax.ShapeDtypeStruct((bn, n_kv, VT_ROWS, ln), BF16),
    )
    assert kv_w == LANES
    head_spec = pl.BlockSpec((1, n_kv, tl, HEAD_DIM), lambda b, i: (b, 0, i, 0))
    vt_spec = pl.BlockSpec((1, n_kv, VT_ROWS, tl), lambda b, i: (b, 0, 0, i))
    return pl.pallas_call(
        kern,
        out_shape=out_shapes,
        grid=(bn, ln // tl),
        in_specs=[
            tok(d),
            pl.BlockSpec((1, 6, d), mod_map),
            full2(g1),
            full2(w_in),
            pl.BlockSpec((tl, LANES), lambda b, i: (i, 0)),
            pl.BlockSpec((tl, LANES), lambda b, i: (i, 0)),
            full2(qg),
            full2(kg),
            full2(bd),
        ],
        out_specs=(tok(d), tok(pool_w), tok(attn_w), tok(kv_w), vt_spec, tok(attn_w), head_spec, vt_spec),
        compiler_params=_cparams(2),
        name="inproj_rope" if rope else "inproj_ctx",
    )(x, mod, g1, w_in, cos, sin, qg, kg, bd)


def _pool_tile(u_ref, up_ref, un_ref, w_ref, s_ref, ln):
    i = pl.program_id(1)
    ni = pl.num_programs(1)
    tl = u_ref.shape[1]
    n_grp = len(POOL_WINDOWS)
    gw = u_ref.shape[2] // n_grp
    ext_len = tl + 2 * POOL_REACH
    up = jnp.where(i > 0, up_ref[0], 0.0)
    un = jnp.where(i < ni - 1, un_ref[0], 0.0)
    t = i * tl + lax.broadcasted_iota(jnp.int32, (tl, 1), 0)
    outs = []
    for gi, win in enumerate(POOL_WINDOWS):
        sl = slice(gi * gw, (gi + 1) * gw)
        u = u_ref[0, :, sl]
        e = jnp.concatenate([up[:, sl], u, un[:, sl]], axis=0)
        half = win // 2
        fwd = e
        step = 1
        while step < half:
            fwd = fwd + pltpu.roll(fwd, ext_len - step, axis=0)
            step *= 2
        wsum = (fwd + pltpu.roll(fwd, half, axis=0))[POOL_REACH:POOL_REACH + tl, :]
        cnt = (jnp.minimum(t + win // 2, ln) - jnp.maximum(t - win // 2, 0)).astype(F32)
        p = wsum / cnt - u
        outs.append((_dot(p.astype(BF16), w_ref[gi]) * s_ref[:, sl]).astype(BF16))
    return jnp.concatenate(outs, axis=1)


def _stack_heads(q, kv):
    return jnp.concatenate(
        [q[:, (kv * Q_PER_KV + g) * HEAD_DIM:(kv * Q_PER_KV + g + 1) * HEAD_DIM] for g in range(Q_PER_KV)], axis=0)


def _rows_to_tokens(ot):
    pad = LANES - ot.shape[0]
    if pad:
        ot = jnp.concatenate([ot, jnp.zeros((pad, ot.shape[1]), F32)], axis=0)
    return ot.T[:, :HEAD_DIM]


def _unstack_heads(o, t):
    return [o[g * t:(g + 1) * t] for g in range(Q_PER_KV)]


def _sink_row(sink_ref, kv, t):
    lane = lax.broadcasted_iota(jnp.int32, (1, Q_PER_KV * t), 1)
    row = jnp.full((1, Q_PER_KV * t), sink_ref[kv * Q_PER_KV + Q_PER_KV - 1] * LOG2E, F32)
    for g in range(Q_PER_KV - 2, -1, -1):
        row = jnp.where(lane < (g + 1) * t, sink_ref[kv * Q_PER_KV + g] * LOG2E, row)
    return row


def _win_kernel(sink_ref, q_ref, kp_ref, kc_ref, kn_ref, vp_ref, vc_ref, vn_ref, kx_ref, vx_ref, o_ref,
                pl_ref, px_ref, es_ref, *, qb, ni):
    t = pl.program_id(0)
    i = jnp.minimum(t, pl.num_programs(0) - 2) % ni
    kall = jnp.concatenate([kp_ref[0], kc_ref[0], kn_ref[0]], axis=0)
    vall = jnp.concatenate([vp_ref[0], vc_ref[0], vn_ref[0]], axis=2)
    kx = kx_ref[0]
    n_kv = kall.shape[1] // HEAD_DIM
    key = lax.broadcasted_iota(jnp.int32, (3 * BLOCK, BLOCK), 0)
    qry = lax.broadcasted_iota(jnp.int32, (3 * BLOCK, BLOCK), 1)
    band = jnp.where(jnp.abs(key - BLOCK - qry) <= WINDOW, 0.0, NEG)
    no_prev = jnp.where((key < BLOCK) & (i == 0), NEG, 0.0)
    no_next = jnp.where((key >= 2 * BLOCK) & (i == ni - 1), NEG, 0.0)

    @pl.when(t == 0)
    def _():
        pl_ref[...] = jnp.zeros_like(pl_ref)
        px_ref[...] = jnp.zeros_like(px_ref)
        es_ref[...] = jnp.ones_like(es_ref)

    blocks = []
    for j in range(qb):
        bias = band
        if j == 0:
            bias = bias + no_prev
        if j == qb - 1:
            bias = bias + no_next
        bias = jnp.concatenate([bias] * Q_PER_KV, axis=1)
        q = q_ref[0, j * BLOCK:(j + 1) * BLOCK, :]
        kwin = kall[j * BLOCK:(j + 3) * BLOCK]
        outs = []
        for kv in range(n_kv):
            g = j * n_kv + kv
            sl = slice(kv * HEAD_DIM, (kv + 1) * HEAD_DIM)
            oa = (_dot(vall[kv, :, j * BLOCK:(j + 3) * BLOCK], pl_ref[g])
                  + _dot(vx_ref[0, kv], px_ref[g]))
            l = oa[HEAD_DIM:HEAD_DIM + 1, :] + es_ref[g]
            outs += _unstack_heads(_rows_to_tokens(oa / l), BLOCK)
            q4 = _stack_heads(q, kv)
            s_lat = _dot_nt(kwin[:, sl], q4) + bias
            s_ctx = _dot_nt(kx[:, sl], q4)
            sink = _sink_row(sink_ref, kv, BLOCK)
            m = jnp.maximum(jnp.maximum(s_lat.max(axis=0, keepdims=True), s_ctx.max(axis=0, keepdims=True)),
                            sink)
            pl_ref[g] = jnp.exp2(s_lat - m).astype(BF16)
            px_ref[g] = jnp.exp2(s_ctx - m).astype(BF16)
            es_ref[g] = jnp.exp2(sink - m)
        blocks.append(jnp.concatenate(outs, axis=1).astype(BF16))
    o_ref[0] = blocks[0] if qb == 1 else jnp.concatenate(blocks, axis=0)


def _window_attention(sink, q, k, vt, kx, vtx, *, qb):
    bn, ln, aw = q.shape
    kw = k.shape[2]
    cn = kx.shape[1]
    n_kv = kw // HEAD_DIM
    nb = ln // BLOCK
    assert kw == LANES and nb % qb == 0
    ni = nb // qb
    n_tiles = bn * ni
    n_grp = qb * n_kv
    mq = Q_PER_KV * BLOCK

    def qk_tile(t):
        t = jnp.minimum(t, n_tiles - 1)
        return t // ni, t % ni

    def pv_tile(t):
        t = jnp.maximum(t - 1, 0)
        return t // ni, t % ni

    def blocks_of(tile):
        cur = lambda t: tile(t)[1]
        prev = lambda t: jnp.maximum(tile(t)[1] * qb - 1, 0)
        nxt = lambda t: jnp.minimum((tile(t)[1] + 1) * qb, nb - 1)
        return cur, prev, nxt

    k_cur, k_prev, k_nxt = blocks_of(qk_tile)
    v_cur, v_prev, v_nxt = blocks_of(pv_tile)
    kspec = lambda rows, f: pl.BlockSpec((1, rows, kw), lambda t: (qk_tile(t)[0], f(t), 0))
    vspec = lambda cols, f: pl.BlockSpec((1, n_kv, VT_ROWS, cols), lambda t: (pv_tile(t)[0], 0, 0, f(t)))
    zero = lambda t: 0
    return pl.pallas_call(
        functools.partial(_win_kernel, qb=qb, ni=ni),
        out_shape=jax.ShapeDtypeStruct((bn, ln, aw), BF16),
        grid=(n_tiles + 1,),
        in_specs=[
            pl.BlockSpec(memory_space=pltpu.SMEM),
            pl.BlockSpec((1, qb * BLOCK, aw), lambda t: (qk_tile(t)[0], k_cur(t), 0)),
            kspec(BLOCK, k_prev), kspec(qb * BLOCK, k_cur), kspec(BLOCK, k_nxt),
            vspec(BLOCK, v_prev), vspec(qb * BLOCK, v_cur), vspec(BLOCK, v_nxt),
            kspec(cn, zero), vspec(cn, zero),
        ],
        out_specs=pl.BlockSpec((1, qb * BLOCK, aw), lambda t: (pv_tile(t)[0], v_cur(t), 0)),
        scratch_shapes=[
            pltpu.VMEM((n_grp, 3 * BLOCK, mq), BF16),
            pltpu.VMEM((n_grp, cn, mq), BF16),
            pltpu.VMEM((n_grp, 1, mq), F32),
        ],
        compiler_params=_cparams(1),
        name="window_attention",
    )(sink, q, k, k, k, vt, vt, vt, kx, vtx)


def _glob_kernel(bound_ref, q_ref, k_ref, v_ref, kx_ref, vx_ref, o_ref, m_ref, p_ref, *, kc):
    s = pl.program_id(0)
    ln = k_ref.shape[2]
    cn = kx_ref.shape[2]
    q = q_ref[0]
    tq = q.shape[0]
    q4 = _stack_heads(q, 0)
    qt = jnp.concatenate([q4.astype(F32), jnp.zeros(q4.shape, F32)], axis=1).T[:HEAD_DIM].astype(BF16)
    pieces = [(k_ref, v_ref, c0, min(kc, ln - c0), c0) for c0 in range(0, ln, kc)] + [(kx_ref, vx_ref, 0, cn, ln)]

    @pl.when(s == 0)
    def _():
        p_ref[...] = jnp.zeros_like(p_ref)

    bound = bound_ref[0]
    m_ref[...] = jnp.full(m_ref.shape, bound - BOUND_SHIFT, F32)

    @pl.when(bound > BOUND_MAX)
    def _():
        m = None
        for kr, _, c0, n, _ in pieces:
            mc = _dot(kr[0, 0, c0:c0 + n, :], qt).max(axis=0, keepdims=True)
            m = mc if m is None else jnp.maximum(m, mc)
        m_ref[...] = m

    m = m_ref[...]
    acc = None
    for kr, vr, c0, n, off in pieces:
        oc = _dot(vr[0, 0, :, c0:c0 + n], p_ref[off:off + n, :])
        acc = oc if acc is None else acc + oc
        p_ref[off:off + n, :] = jnp.exp2(_dot(kr[0, 0, c0:c0 + n, :], qt) - m).astype(BF16)
    l = acc[HEAD_DIM:HEAD_DIM + 1, :]
    l = jnp.where(l > 0.0, l, 1.0)
    o = _rows_to_tokens(acc / l)
    o_ref[0] = jnp.concatenate(_unstack_heads(o, tq), axis=1).astype(BF16)


def _score_bound(qg, kg):
    return (HEAD_DIM * Q_SCALE * BOUND_SLACK * jnp.max(jnp.abs(qg)) * jnp.max(jnp.abs(kg))).reshape(1)


def _global_attention(bound, q, k, v, kx, vx, *, tq, kc):
    bn, ln, aw = q.shape
    n_kv = k.shape[1]
    cn = kx.shape[2]
    gw = aw // n_kv
    assert gw == Q_PER_KV * HEAD_DIM
    nq = ln // tq
    n_tiles = bn * n_kv * nq

    def split(t):
        return t // (n_kv * nq), (t // nq) % n_kv, t % nq

    def qk_tile(s):
        return split(jnp.minimum(s, n_tiles - 1))

    def pv_tile(s):
        return split(jnp.maximum(s - 1, 0))

    def q_map(s):
        b, h, i = qk_tile(s)
        return b, i, h

    def k_map(s):
        b, h, _ = qk_tile(s)
        return b, h, 0, 0

    def v_map(s):
        b, h, _ = pv_tile(s)
        return b, h, 0, 0

    def o_map(s):
        b, h, i = pv_tile(s)
        return b, i, h

    return pl.pallas_call(
        functools.partial(_glob_kernel, kc=kc),
        out_shape=jax.ShapeDtypeStruct((bn, ln, aw), BF16),
        grid=(n_tiles + 1,),
        in_specs=[
            pl.BlockSpec(memory_space=pltpu.SMEM),
            pl.BlockSpec((1, tq, gw), q_map),
            pl.BlockSpec((1, 1, ln, HEAD_DIM), k_map),
            pl.BlockSpec((1, 1, VT_ROWS, ln), v_map),
            pl.BlockSpec((1, 1, cn, HEAD_DIM), k_map),
            pl.BlockSpec((1, 1, VT_ROWS, cn), v_map),
        ],
        out_specs=pl.BlockSpec((1, tq, gw), o_map),
        scratch_shapes=[pltpu.VMEM((1, Q_PER_KV * tq), F32), pltpu.VMEM((ln + cn, Q_PER_KV * tq), BF16)],
        compiler_params=_cparams(1),
        name="global_attention",
    )(bound, q, k, v, kx, vx)


def _ctx_attn_kernel(sink_ref, wq_ref, wk_ref, wv_ref, gq_ref, gk_ref, gv_ref, ow_ref, og_ref):
    wq = wq_ref[0]
    gq = gq_ref[0]
    wk = wk_ref[0]
    cn, kw = wk.shape
    outs_w = []
    outs_g = []
    for kv in range(kw // HEAD_DIM):
        st = _dot_nt(wk[:, kv * HEAD_DIM:(kv + 1) * HEAD_DIM], _stack_heads(wq, kv))
        sink = _sink_row(sink_ref, kv, cn)
        m = jnp.maximum(st.max(axis=0, keepdims=True), sink)
        oa = _dot(wv_ref[0, kv], jnp.exp2(st - m).astype(BF16))
        l = oa[HEAD_DIM:HEAD_DIM + 1, :] + jnp.exp2(sink - m)
        outs_w += _unstack_heads(_rows_to_tokens(oa / l), cn)
        st = _dot_nt(gk_ref[0, kv], _stack_heads(gq, kv))
        m = st.max(axis=0, keepdims=True)
        oa = _dot(gv_ref[0, kv], jnp.exp2(st - m).astype(BF16))
        outs_g += _unstack_heads(_rows_to_tokens(oa / oa[HEAD_DIM:HEAD_DIM + 1, :]), cn)
    ow_ref[0] = jnp.concatenate(outs_w, axis=1).astype(BF16)
    og_ref[0] = jnp.concatenate(outs_g, axis=1).astype(BF16)


def _context_attention(sink, wq, wk, wv, gq, gk, gv):
    bn, cn, aw = wq.shape
    kw = wk.shape[2]
    n_kv = gk.shape[1]
    tok = lambda w: pl.BlockSpec((1, cn, w), lambda b: (b, 0, 0))
    hd = lambda r, w: pl.BlockSpec((1, n_kv, r, w), lambda b: (b, 0, 0, 0))
    shp = jax.ShapeDtypeStruct((bn, cn, aw), BF16)
    return pl.pallas_call(
        _ctx_attn_kernel,
        out_shape=(shp, shp),
        grid=(bn,),
        in_specs=[pl.BlockSpec(memory_space=pltpu.SMEM), tok(aw), tok(kw), hd(VT_ROWS, cn), tok(aw),
                  hd(cn, HEAD_DIM), hd(VT_ROWS, cn)],
        out_specs=(tok(aw), tok(aw)),
        compiler_params=_cparams(1),
        name="context_attention",
    )(sink, wq, wk, wv, gq, gk, gv)


def _merge_kernel(x_ref, h_ref, mod_ref, g2_ref, u_ref, up_ref, un_ref, wp_ref, ps_ref, yw_ref, yg_ref,
                  wg_ref, bg_ref, wb_ref, wo_ref, x1_ref, h2_ref, *, ln):
    x = x_ref[0]
    h = h_ref[0]
    ys = ((1, yw_ref[0]), (2, yg_ref[0]), (0, _pool_tile(u_ref, up_ref, un_ref, wp_ref, ps_ref, ln)))
    merged = None
    for i, y in ys:
        gate = jax.nn.sigmoid(_dot(h, wg_ref[i]) + bg_ref[i:i + 1, :])
        term = gate * _dot(y, wb_ref[i])
        merged = term if merged is None else merged + term
    x1 = x + mod_ref[0, 2:3, :] * _dot(merged.astype(BF16), wo_ref[...])
    x1_ref[0] = x1
    h2_ref[0] = _rms_mod(x1, g2_ref[...], mod_ref[0, 3:4, :], mod_ref[0, 4:5, :]).astype(BF16)


def _merge(x, h, mod, g2, u, w_grp, p_scale, yw, yg, wg, bg, wb, wo, *, tl):
    bn, ln, d = x.shape
    bw = yw.shape[2]
    r = tl // POOL_REACH
    n_halo = ln // POOL_REACH
    halo = lambda f: pl.BlockSpec((1, POOL_REACH, bw), f)
    mod_map = (lambda b, i: (b, 0, 0)) if mod.shape[0] > 1 else (lambda b, i: (0, 0, 0))
    tok = lambda w: pl.BlockSpec((1, tl, w), lambda b, i: (b, i, 0))
    full = _resident
    return pl.pallas_call(
        functools.partial(_merge_kernel, ln=ln),
        out_shape=(jax.ShapeDtypeStruct((bn, ln, d), F32), jax.ShapeDtypeStruct((bn, ln, d), BF16)),
        grid=(bn, ln // tl),
        in_specs=[tok(d), tok(d), pl.BlockSpec((1, 6, d), mod_map), full(g2),
                  tok(bw),
                  halo(lambda b, i: (b, jnp.maximum(i * r - 1, 0), 0)),
                  halo(lambda b, i: (b, jnp.minimum((i + 1) * r, n_halo - 1), 0)),
                  full(w_grp), full(p_scale), tok(bw), tok(bw),
                  full(wg), full(bg), full(wb), full(wo)],
        out_specs=(tok(d), tok(d)),
        compiler_params=_cparams(2),
        name="merge_branches",
    )(x, h, mod, g2, u, u, u, w_grp, p_scale, yw, yg, wg, bg, wb, wo)


def _ffn_kernel(h_ref, hp_ref, hn_ref, x_ref, mod_ref, wg_ref, wv_ref, cw_ref, cb_ref, wd_ref, fg_ref,
                o_ref, g_scr, *, final):
    j = pl.program_id(1)
    nj = pl.num_programs(1)
    tl = h_ref.shape[1]
    halo = hp_ref.shape[1]
    hm = h_ref[0]
    hp = jnp.where(j > 0, hp_ref[0], jnp.zeros_like(hp_ref[0]))
    hn = jnp.where(j < nj - 1, hn_ref[0], jnp.zeros_like(hn_ref[0]))
    hext = jnp.concatenate([hp, hm, hn], axis=0)
    g_scr[...] = _dot(hext, wg_ref[...])
    a = (g_scr[pl.ds(halo - 1, tl), :] * cw_ref[0:1, :]
         + g_scr[pl.ds(halo, tl), :] * cw_ref[1:2, :]
         + g_scr[pl.ds(halo + 1, tl), :] * cw_ref[2:3, :]
         + cb_ref[...])
    val = _dot(hm, wv_ref[...])
    act = (a * jax.nn.sigmoid(a)) * val
    x2 = x_ref[0] + mod_ref[0, 5:6, :] * _dot(act.astype(BF16), wd_ref[...])
    if final:
        x2 = x2 * lax.rsqrt(jnp.mean(x2 * x2, axis=-1, keepdims=True) + EPS) * fg_ref[...]
    o_ref[0] = x2


def _ffn(h2, x1, mod, wg, wv, cw, cb, wd, fg, *, tl, final):
    bn, ln, d = x1.shape
    f = wg.shape[1]
    halo = BF16_SUBLANES
    r = tl // halo
    nh = ln // halo
    mod_map = (lambda b, i: (b, 0, 0)) if mod.shape[0] > 1 else (lambda b, i: (0, 0, 0))
    tok = pl.BlockSpec((1, tl, d), lambda b, i: (b, i, 0))
    full = _resident
    return pl.pallas_call(
        functools.partial(_ffn_kernel, final=final),
        out_shape=jax.ShapeDtypeStruct((bn, ln, d), F32),
        grid=(bn, ln // tl),
        in_specs=[
            tok,
            pl.BlockSpec((1, halo, d), lambda b, i: (b, jnp.maximum(i * r - 1, 0), 0)),
            pl.BlockSpec((1, halo, d), lambda b, i: (b, jnp.minimum((i + 1) * r, nh - 1), 0)),
            tok,
            pl.BlockSpec((1, 6, d), mod_map),
            full(wg), full(wv), full(cw), full(cb), full(wd), full(fg),
        ],
        out_specs=tok,
        scratch_shapes=[pltpu.VMEM((tl + 2 * halo, f), F32)],
        compiler_params=_cparams(2),
        name="conv_glu_final" if final else "conv_glu",
    )(h2, h2, h2, x1, mod, wg, wv, cw, cb, wd, fg)


def _rope_tables(n_tok):
    pos = jnp.arange(n_tok)
    row = (pos // GRID_W).astype(F32)
    col = (pos % GRID_W).astype(F32)
    half = HEAD_DIM // 2
    inv = ROPE_THETA ** (-jnp.arange(0, half, 2, dtype=F32) / half)
    ang_r = row[:, None] * inv
    ang_c = col[:, None] * inv
    cos64 = jnp.concatenate([jnp.cos(ang_r)] * 2 + [jnp.cos(ang_c)] * 2, axis=1)
    sin64 = jnp.concatenate([-jnp.sin(ang_r), jnp.sin(ang_r), -jnp.sin(ang_c), jnp.sin(ang_c)], axis=1)
    reps = LANES // HEAD_DIM
    return jnp.tile(cos64, (1, reps)), jnp.tile(sin64, (1, reps))


def _tile_rows(n, target):
    t = min(n, target)
    while n % t:
        t //= 2
    return t


def kernel(x, c, ctx, c_ctx, w_mod, b_mod, norm1_g, norm2_g, w_in, w_pool_grp, pool_scale, win_sink,
           q_norm_g, k_norm_g, w_branch, w_gate, b_gate, w_out, w_ff_gate, w_ff_val, conv_w, conv_b,
           w_ff_down, final_g):
    bn, ln, d = x.shape
    cn = ctx.shape[1]
    depth = w_mod.shape[0]
    attn_w = d // 2
    kv_w = attn_w // Q_PER_KV

    mod_all = _modulation(jnp.concatenate([c, c_ctx[None]], axis=0), w_mod, b_mod)
    mod_all = mod_all.reshape(depth, bn + 1, 6, d)

    cos, sin = _rope_tables(ln)
    cos_c = jnp.ones((cn, LANES), F32)
    sin_c = jnp.zeros((cn, LANES), F32)
    ids = np.arange(2 * LANES) // HEAD_DIM
    bd = jnp.asarray(ids[:, None] == ids[None, :], dtype=BF16)
    fg = final_g.reshape(1, d)

    tl = _tile_rows(ln, DENSE_TILE)
    tl_ffn = _tile_rows(ln, FFN_TILE)
    tl_c = _tile_rows(cn, CTX_TILE)
    tq = _tile_rows(ln, GLOBAL_QUERY_TILE)
    win_qb = _tile_rows(ln // BLOCK, WINDOW_BLOCKS_PER_STEP)

    for l in range(depth):
        last = l == depth - 1
        mod_l = mod_all[l, :bn]
        mod_c = mod_all[l, bn:]
        g1 = norm1_g[l].reshape(1, d)
        g2 = norm2_g[l].reshape(1, d)
        w_in_l = w_in[l].astype(BF16)
        qg = jnp.tile(q_norm_g[l], attn_w // HEAD_DIM).reshape(1, attn_w)
        kg = jnp.tile(k_norm_g[l], kv_w // HEAD_DIM).reshape(1, kv_w)
        w_grp = w_pool_grp[l].astype(BF16)
        p_scale = pool_scale[l].reshape(1, -1)
        sink = win_sink[l]
        wg = w_gate[l].astype(BF16)
        bg = b_gate[l]
        wb = w_branch[l].astype(BF16)
        wo = w_out[l].astype(BF16)
        wfg = w_ff_gate[l].astype(BF16)
        wfv = w_ff_val[l].astype(BF16)
        wfd = w_ff_down[l].astype(BF16)
        cw = conv_w[l]
        cb = conv_b[l].reshape(1, -1)

        h_l, u_l, wq_l, wk_l, wv_l, gq_l, gk_l, gv_l = _inproj(
            x, mod_l, g1, w_in_l, cos, sin, qg, kg, bd, rope=True, tl=tl)
        h_c, u_c, wq_c, wk_c, wv_c, gq_c, gk_c, gv_c = _inproj(
            ctx, mod_c, g1, w_in_l, cos_c, sin_c, qg, kg, bd, rope=False, tl=tl_c)

        y_win = _window_attention(sink, wq_l, wk_l, wv_l, wk_c, wv_c, qb=win_qb)
        bound = _score_bound(q_norm_g[l], k_norm_g[l])
        y_glob = _global_attention(bound, gq_l, gk_l, gv_l, gk_c, gv_c, tq=tq, kc=GLOBAL_KEY_CHUNK)
        x1, h2 = _merge(x, h_l, mod_l, g2, u_l, w_grp, p_scale, y_win, y_glob, wg, bg, wb, wo, tl=tl)

        if not last:
            yc_win, yc_glob = _context_attention(sink, wq_c, wk_c, wv_c, gq_c, gk_c, gv_c)
            c1, hc2 = _merge(ctx, h_c, mod_c, g2, u_c, w_grp, p_scale, yc_win, yc_glob, wg, bg, wb, wo, tl=tl_c)
            ctx = _ffn(hc2, c1, mod_c, wfg, wfv, cw, cb, wfd, fg, tl=tl_c, final=False)

        x = _ffn(h2, x1, mod_l, wfg, wfv, cw, cb, wfd, fg, tl=tl_ffn, final=last)
    return x
```

```python
import functools

import jax
import jax.numpy as jnp
import numpy as np
from jax import lax
from jax.experimental import pallas as pl
from jax.experimental.pallas import tpu as pltpu

F32 = jnp.float32
BF16 = jnp.bfloat16

GRID_W = 64
HEAD_DIM = 64
Q_PER_KV = 4
POOL_WINDOWS = (2, 4, 8, 16)
POOL_REACH = max(POOL_WINDOWS) // 2
WINDOW = 128
BLOCK = 128
ROPE_THETA = 10000.0
EPS = 1e-6
NEG = -1e30
SM_SCALE = HEAD_DIM ** -0.5
LOG2E = 1.4426950408889634
Q_SCALE = SM_SCALE * LOG2E
BOUND_SLACK = 1.02
BOUND_SHIFT = 60.0
BOUND_MAX = 80.0
LANES = 128
VT_ROWS = 128
BF16_SUBLANES = 16
VMEM_LIMIT = 56 * 1024 * 1024

DENSE_TILE = 1024
FFN_TILE = 512
CTX_TILE = 256
GLOBAL_QUERY_TILE = 512
GLOBAL_KEY_CHUNK = 1024
WINDOW_BLOCKS_PER_STEP = 8
MOD_COLS = 1536


def _cparams(n_axes):
    return pltpu.CompilerParams(
        dimension_semantics=("arbitrary",) * n_axes,
        vmem_limit_bytes=VMEM_LIMIT,
    )


def _resident(a):
    return pl.BlockSpec(a.shape, lambda *_: (0,) * a.ndim, pipeline_mode=pl.Buffered(1))


def _rms_mod(x, g, shift, scale):
    y = x * lax.rsqrt(jnp.mean(x * x, axis=-1, keepdims=True) + EPS)
    return (y * g) * (1.0 + scale) + shift


def _dot(a, b):
    return jnp.dot(a, b, preferred_element_type=F32)


def _dot_nt(a, b):
    return lax.dot_general(a, b, (((1,), (1,)), ((), ())), preferred_element_type=F32)


def _mod_kernel(c_ref, w_ref, b_ref, o_ref):
    c = c_ref[...]
    s = c * jax.nn.sigmoid(c)
    o_ref[0] = _dot(s.astype(BF16), w_ref[0].astype(BF16)) + b_ref[0]


def _modulation(cc, w_mod, b_mod):
    depth, d, n = w_mod.shape
    rows = cc.shape[0]
    tn = _tile_rows(n, MOD_COLS)
    return pl.pallas_call(
        _mod_kernel,
        out_shape=jax.ShapeDtypeStruct((depth, rows, n), F32),
        grid=(depth, n // tn),
        in_specs=[
            pl.BlockSpec((rows, d), lambda l, j: (0, 0)),
            pl.BlockSpec((1, d, tn), lambda l, j: (l, 0, j)),
            pl.BlockSpec((1, 1, tn), lambda l, j: (l, 0, j)),
        ],
        out_specs=pl.BlockSpec((1, rows, tn), lambda l, j: (l, 0, j)),
        compiler_params=_cparams(2),
        name="modulation",
    )(cc, w_mod, b_mod.reshape(depth, 1, n))


def _head_rms(t, bd, g):
    sq = t * t
    hi = sq.astype(BF16)
    lo = (sq - hi.astype(F32)).astype(BF16)
    w = t.shape[1]
    parts = []
    for j in range(0, w, 2 * LANES):
        e = min(j + 2 * LANES, w)
        b = bd[: e - j, : e - j]
        parts.append(_dot(hi[:, j:e], b) + _dot(lo[:, j:e], b))
    ssq = parts[0] if len(parts) == 1 else jnp.concatenate(parts, axis=1)
    return t * lax.rsqrt(ssq * (1.0 / HEAD_DIM) + EPS) * g


def _rope(t, cos, sin, partner_lane):
    outs = []
    for j in range(0, t.shape[1], LANES):
        tj = t[:, j:j + LANES]
        partner = jnp.take_along_axis(tj, partner_lane, axis=1)
        outs.append(tj * cos + partner * sin)
    return outs[0] if len(outs) == 1 else jnp.concatenate(outs, axis=1)


def _inproj_kernel(x_ref, mod_ref, g1_ref, w_ref, cos_ref, sin_ref, qg_ref, kg_ref, bd_ref,
                   h_ref, u_ref, wq_ref, wk_ref, wv_ref, gq_ref, gk_ref, gv_ref, *, rope, pool_w, attn_w, kv_w):
    x = x_ref[0]
    h = _rms_mod(x, g1_ref[...], mod_ref[0, 0:1, :], mod_ref[0, 1:2, :])
    h = h.astype(BF16)
    h_ref[0] = h
    mix_w = attn_w + 2 * kv_w
    zg = _dot(h, w_ref[:, pool_w + mix_w:pool_w + 2 * mix_w])
    zw = _dot(h, w_ref[:, pool_w:pool_w + mix_w])
    u_ref[0] = _dot(h, w_ref[:, :pool_w])
    gq, gk, gv = zg[:, :attn_w], zg[:, attn_w:attn_w + kv_w], zg[:, attn_w + kv_w:]
    wq, wk, wv = zw[:, :attn_w], zw[:, attn_w:attn_w + kv_w], zw[:, attn_w + kv_w:]
    bd = bd_ref[...]
    gq = _head_rms(gq, bd, qg_ref[...])
    gk = _head_rms(gk, bd, kg_ref[...])
    if rope:
        cos = cos_ref[...]
        sin = sin_ref[...]
        lane = lax.broadcasted_iota(jnp.int32, cos.shape, 1)
        partner_lane = lane ^ 16
        wq = _rope(wq, cos, sin, partner_lane)
        wk = _rope(wk, cos, sin, partner_lane)
        gq = _rope(gq, cos, sin, partner_lane)
        gk = _rope(gk, cos, sin, partner_lane)
    wq_ref[0] = (wq * Q_SCALE).astype(BF16)
    gq_ref[0] = (gq * Q_SCALE).astype(BF16)
    wk_ref[0] = wk.astype(BF16)
    tail_row = lax.broadcasted_iota(jnp.int32, (VT_ROWS - HEAD_DIM, gv.shape[0]), 0)
    tail = jnp.where(tail_row == 0, 1.0, 0.0).astype(BF16)
    for v, v_ref in ((wv, wv_ref), (gv, gv_ref)):
        vt = v.T
        for k in range(kv_w // HEAD_DIM):
            v_ref[0, k, 0:HEAD_DIM, :] = vt[k * HEAD_DIM:(k + 1) * HEAD_DIM].astype(BF16)
            v_ref[0, k, HEAD_DIM:VT_ROWS, :] = tail
    for k in range(kv_w // HEAD_DIM):
        gk_ref[0, k] = gk[:, k * HEAD_DIM:(k + 1) * HEAD_DIM].astype(BF16)


def _inproj(x, mod, g1, w_in, cos, sin, qg, kg, bd, *, rope, tl):
    bn, ln, d = x.shape
    in_w = w_in.shape[1]
    attn_w = qg.shape[1]
    kv_w = kg.shape[1]
    pool_w = in_w - 2 * attn_w - 4 * kv_w
    n_kv = kv_w // HEAD_DIM
    mod_b = mod.shape[0]
    mod_map = (lambda b, i: (b, 0, 0)) if mod_b > 1 else (lambda b, i: (0, 0, 0))
    tok = lambda w: pl.BlockSpec((1, tl, w), lambda b, i: (b, i, 0))
    full2 = _resident
    kern = functools.partial(_inproj_kernel, rope=rope, pool_w=pool_w, attn_w=attn_w, kv_w=kv_w)
    out_shapes = (
        jax.ShapeDtypeStruct((bn, ln, d), BF16),
        jax.ShapeDtypeStruct((bn, ln, pool_w), F32),
        jax.ShapeDtypeStruct((bn, ln, attn_w), BF16),
        jax.ShapeDtypeStruct((bn, ln, kv_w), BF16),
        jax.ShapeDtypeStruct((bn, n_kv, VT_ROWS, ln), BF16),
        jax.ShapeDtypeStruct((bn, ln, attn_w), BF16),
        jax.ShapeDtypeStruct((bn, n_kv, ln, HEAD_DIM), BF16),
        jax.ShapeDtypeStruct((bn, n_kv, VT_ROWS, ln), BF16),
    )
    assert kv_w == LANES
    head_spec = pl.BlockSpec((1, n_kv, tl, HEAD_DIM), lambda b, i: (b, 0, i, 0))
    vt_spec = pl.BlockSpec((1, n_kv, VT_ROWS, tl), lambda b, i: (b, 0, 0, i))
    return pl.pallas_call(
        kern,
        out_shape=out_shapes,
        grid=(bn, ln // tl),
        in_specs=[
            tok(d),
            pl.BlockSpec((1, 6, d), mod_map),
            full2(g1),
            full2(w_in),
            pl.BlockSpec((tl, LANES), lambda b, i: (i, 0)),
            pl.BlockSpec((tl, LANES), lambda b, i: (i, 0)),
            full2(qg),
            full2(kg),
            full2(bd),
        ],
        out_specs=(tok(d), tok(pool_w), tok(attn_w), tok(kv_w), vt_spec, tok(attn_w), head_spec, vt_spec),
        compiler_params=_cparams(2),
        name="inproj_rope" if rope else "inproj_ctx",
    )(x, mod, g1, w_in, cos, sin, qg, kg, bd)


def _pool_tile(u_ref, up_ref, un_ref, w_ref, s_ref, ln):
    i = pl.program_id(1)
    ni = pl.num_programs(1)
    tl = u_ref.shape[1]
    n_grp = len(POOL_WINDOWS)
    gw = u_ref.shape[2] // n_grp
    ext_len = tl + 2 * POOL_REACH
    up = jnp.where(i > 0, up_ref[0], 0.0)
    un = jnp.where(i < ni - 1, un_ref[0], 0.0)
    t = i * tl + lax.broadcasted_iota(jnp.int32, (tl, 1), 0)
    outs = []
    for gi, win in enumerate(POOL_WINDOWS):
        sl = slice(gi * gw, (gi + 1) * gw)
        u = u_ref[0, :, sl]
        e = jnp.concatenate([up[:, sl], u, un[:, sl]], axis=0)
        half = win // 2
        fwd = e
        step = 1
        while step < half:
            fwd = fwd + pltpu.roll(fwd, ext_len - step, axis=0)
            step *= 2
        wsum = (fwd + pltpu.roll(fwd, half, axis=0))[POOL_REACH:POOL_REACH + tl, :]
        cnt = (jnp.minimum(t + win // 2, ln) - jnp.maximum(t - win // 2, 0)).astype(F32)
        p = wsum / cnt - u
        outs.append((_dot(p.astype(BF16), w_ref[gi]) * s_ref[:, sl]).astype(BF16))
    return jnp.concatenate(outs, axis=1)


def _stack_heads(q, kv):
    return jnp.concatenate(
        [q[:, (kv * Q_PER_KV + g) * HEAD_DIM:(kv * Q_PER_KV + g + 1) * HEAD_DIM] for g in range(Q_PER_KV)], axis=0)


def _rows_to_tokens(ot):
    pad = LANES - ot.shape[0]
    if pad:
        ot = jnp.concatenate([ot, jnp.zeros((pad, ot.shape[1]), F32)], axis=0)
    return ot.T[:, :HEAD_DIM]


def _unstack_heads(o, t):
    return [o[g * t:(g + 1) * t] for g in range(Q_PER_KV)]


def _sink_row(sink_ref, kv, t):
    lane = lax.broadcasted_iota(jnp.int32, (1, Q_PER_KV * t), 1)
    row = jnp.full((1, Q_PER_KV * t), sink_ref[kv * Q_PER_KV + Q_PER_KV - 1] * LOG2E, F32)
    for g in range(Q_PER_KV - 2, -1, -1):
        row = jnp.where(lane < (g + 1) * t, sink_ref[kv * Q_PER_KV + g] * LOG2E, row)
    return row


def _win_kernel(sink_ref, q_ref, kp_ref, kc_ref, kn_ref, vp_ref, vc_ref, vn_ref, kx_ref, vx_ref, o_ref,
                pl_ref, px_ref, es_ref, *, qb, ni):
    t = pl.program_id(0)
    i = jnp.minimum(t, pl.num_programs(0) - 2) % ni
    kall = jnp.concatenate([kp_ref[0], kc_ref[0], kn_ref[0]], axis=0)
    vall = jnp.concatenate([vp_ref[0], vc_ref[0], vn_ref[0]], axis=2)
    kx = kx_ref[0]
    n_kv = kall.shape[1] // HEAD_DIM
    key = lax.broadcasted_iota(jnp.int32, (3 * BLOCK, BLOCK), 0)
    qry = lax.broadcasted_iota(jnp.int32, (3 * BLOCK, BLOCK), 1)
    band = jnp.where(jnp.abs(key - BLOCK - qry) <= WINDOW, 0.0, NEG)
    no_prev = jnp.where((key < BLOCK) & (i == 0), NEG, 0.0)
    no_next = jnp.where((key >= 2 * BLOCK) & (i == ni - 1), NEG, 0.0)

    @pl.when(t == 0)
    def _():
        pl_ref[...] = jnp.zeros_like(pl_ref)
        px_ref[...] = jnp.zeros_like(px_ref)
        es_ref[...] = jnp.ones_like(es_ref)

    blocks = []
    for j in range(qb):
        bias = band
        if j == 0:
            bias = bias + no_prev
        if j == qb - 1:
            bias = bias + no_next
        bias = jnp.concatenate([bias] * Q_PER_KV, axis=1)
        q = q_ref[0, j * BLOCK:(j + 1) * BLOCK, :]
        kwin = kall[j * BLOCK:(j + 3) * BLOCK]
        outs = []
        for kv in range(n_kv):
            g = j * n_kv + kv
            sl = slice(kv * HEAD_DIM, (kv + 1) * HEAD_DIM)
            oa = (_dot(vall[kv, :, j * BLOCK:(j + 3) * BLOCK], pl_ref[g])
                  + _dot(vx_ref[0, kv], px_ref[g]))
            l = oa[HEAD_DIM:HEAD_DIM + 1, :] + es_ref[g]
            outs += _unstack_heads(_rows_to_tokens(oa / l), BLOCK)
            q4 = _stack_heads(q, kv)
            s_lat = _dot_nt(kwin[:, sl], q4) + bias
            s_ctx = _dot_nt(kx[:, sl], q4)
            sink = _sink_row(sink_ref, kv, BLOCK)
            m = jnp.maximum(jnp.maximum(s_lat.max(axis=0, keepdims=True), s_ctx.max(axis=0, keepdims=True)),
                            sink)
            pl_ref[g] = jnp.exp2(s_lat - m).astype(BF16)
            px_ref[g] = jnp.exp2(s_ctx - m).astype(BF16)
            es_ref[g] = jnp.exp2(sink - m)
        blocks.append(jnp.concatenate(outs, axis=1).astype(BF16))
    o_ref[0] = blocks[0] if qb == 1 else jnp.concatenate(blocks, axis=0)


def _window_attention(sink, q, k, vt, kx, vtx, *, qb):
    bn, ln, aw = q.shape
    kw = k.shape[2]
    cn = kx.shape[1]
    n_kv = kw // HEAD_DIM
    nb = ln // BLOCK
    assert kw == LANES and nb % qb == 0
    ni = nb // qb
    n_tiles = bn * ni
    n_grp = qb * n_kv
    mq = Q_PER_KV * BLOCK

    def qk_tile(t):
        t = jnp.minimum(t, n_tiles - 1)
        return t // ni, t % ni

    def pv_tile(t):
        t = jnp.maximum(t - 1, 0)
        return t // ni, t % ni

    def blocks_of(tile):
        cur = lambda t: tile(t)[1]
        prev = lambda t: jnp.maximum(tile(t)[1] * qb - 1, 0)
        nxt = lambda t: jnp.minimum((tile(t)[1] + 1) * qb, nb - 1)
        return cur, prev, nxt

    k_cur, k_prev, k_nxt = blocks_of(qk_tile)
    v_cur, v_prev, v_nxt = blocks_of(pv_tile)
    kspec = lambda rows, f: pl.BlockSpec((1, rows, kw), lambda t: (qk_tile(t)[0], f(t), 0))
    vspec = lambda cols, f: pl.BlockSpec((1, n_kv, VT_ROWS, cols), lambda t: (pv_tile(t)[0], 0, 0, f(t)))
    zero = lambda t: 0
    return pl.pallas_call(
        functools.partial(_win_kernel, qb=qb, ni=ni),
        out_shape=jax.ShapeDtypeStruct((bn, ln, aw), BF16),
        grid=(n_tiles + 1,),
        in_specs=[
            pl.BlockSpec(memory_space=pltpu.SMEM),
            pl.BlockSpec((1, qb * BLOCK, aw), lambda t: (qk_tile(t)[0], k_cur(t), 0)),
            kspec(BLOCK, k_prev), kspec(qb * BLOCK, k_cur), kspec(BLOCK, k_nxt),
            vspec(BLOCK, v_prev), vspec(qb * BLOCK, v_cur), vspec(BLOCK, v_nxt),
            kspec(cn, zero), vspec(cn, zero),
        ],
        out_specs=pl.BlockSpec((1, qb * BLOCK, aw), lambda t: (pv_tile(t)[0], v_cur(t), 0)),
        scratch_shapes=[
            pltpu.VMEM((n_grp, 3 * BLOCK, mq), BF16),
            pltpu.VMEM((n_grp, cn, mq), BF16),
            pltpu.VMEM((n_grp, 1, mq), F32),
        ],
        compiler_params=_cparams(1),
        name="window_attention",
    )(sink, q, k, k, k, vt, vt, vt, kx, vtx)


def _glob_kernel(bound_ref, q_ref, k_ref, v_ref, kx_ref, vx_ref, o_ref, m_ref, p_ref, *, kc):
    s = pl.program_id(0)
    ln = k_ref.shape[2]
    cn = kx_ref.shape[2]
    q = q_ref[0]
    tq = q.shape[0]
    q4 = _stack_heads(q, 0)
    pieces = [(k_ref, v_ref, c0, min(kc, ln - c0), c0) for c0 in range(0, ln, kc)] + [(kx_ref, vx_ref, 0, cn, ln)]

    @pl.when(s == 0)
    def _():
        p_ref[...] = jnp.zeros_like(p_ref)

    bound = bound_ref[0]
    m_ref[...] = jnp.full(m_ref.shape, bound - BOUND_SHIFT, F32)

    @pl.when(bound > BOUND_MAX)
    def _():
        m = None
        for kr, _, c0, n, _ in pieces:
            mc = _dot_nt(kr[0, 0, c0:c0 + n, :], q4).max(axis=0, keepdims=True)
            m = mc if m is None else jnp.maximum(m, mc)
        m_ref[...] = m

    m = m_ref[...]
    acc = None
    for kr, vr, c0, n, off in pieces:
        oc = _dot(vr[0, 0, :, c0:c0 + n], p_ref[off:off + n, :])
        acc = oc if acc is None else acc + oc
        p_ref[off:off + n, :] = jnp.exp2(_dot_nt(kr[0, 0, c0:c0 + n, :], q4) - m).astype(BF16)
    l = acc[HEAD_DIM:HEAD_DIM + 1, :]
    l = jnp.where(l > 0.0, l, 1.0)
    o = _rows_to_tokens(acc / l)
    o_ref[0] = jnp.concatenate(_unstack_heads(o, tq), axis=1).astype(BF16)


def _score_bound(qg, kg):
    return (HEAD_DIM * Q_SCALE * BOUND_SLACK * jnp.max(jnp.abs(qg)) * jnp.max(jnp.abs(kg))).reshape(1)


def _global_attention(bound, q, k, v, kx, vx, *, tq, kc):
    bn, ln, aw = q.shape
    n_kv = k.shape[1]
    cn = kx.shape[2]
    gw = aw // n_kv
    assert gw == Q_PER_KV * HEAD_DIM
    nq = ln // tq
    n_tiles = bn * n_kv * nq

    def split(t):
        return t // (n_kv * nq), (t // nq) % n_kv, t % nq

    def qk_tile(s):
        return split(jnp.minimum(s, n_tiles - 1))

    def pv_tile(s):
        return split(jnp.maximum(s - 1, 0))

    def q_map(s):
        b, h, i = qk_tile(s)
        return b, i, h

    def k_map(s):
        b, h, _ = qk_tile(s)
        return b, h, 0, 0

    def v_map(s):
        b, h, _ = pv_tile(s)
        return b, h, 0, 0

    def o_map(s):
        b, h, i = pv_tile(s)
        return b, i, h

    return pl.pallas_call(
        functools.partial(_glob_kernel, kc=kc),
        out_shape=jax.ShapeDtypeStruct((bn, ln, aw), BF16),
        grid=(n_tiles + 1,),
        in_specs=[
            pl.BlockSpec(memory_space=pltpu.SMEM),
            pl.BlockSpec((1, tq, gw), q_map),
            pl.BlockSpec((1, 1, ln, HEAD_DIM), k_map),
            pl.BlockSpec((1, 1, VT_ROWS, ln), v_map),
            pl.BlockSpec((1, 1, cn, HEAD_DIM), k_map),
            pl.BlockSpec((1, 1, VT_ROWS, cn), v_map),
        ],
        out_specs=pl.BlockSpec((1, tq, gw), o_map),
        scratch_shapes=[pltpu.VMEM((1, Q_PER_KV * tq), F32), pltpu.VMEM((ln + cn, Q_PER_KV * tq), BF16)],
        compiler_params=_cparams(1),
        name="global_attention",
    )(bound, q, k, v, kx, vx)


def _ctx_attn_kernel(sink_ref, wq_ref, wk_ref, wv_ref, gq_ref, gk_ref, gv_ref, ow_ref, og_ref):
    wq = wq_ref[0]
    gq = gq_ref[0]
    wk = wk_ref[0]
    cn, kw = wk.shape
    outs_w = []
    outs_g = []
    for kv in range(kw // HEAD_DIM):
        st = _dot_nt(wk[:, kv * HEAD_DIM:(kv + 1) * HEAD_DIM], _stack_heads(wq, kv))
        sink = _sink_row(sink_ref, kv, cn)
        m = jnp.maximum(st.max(axis=0, keepdims=True), sink)
        oa = _dot(wv_ref[0, kv], jnp.exp2(st - m).astype(BF16))
        l = oa[HEAD_DIM:HEAD_DIM + 1, :] + jnp.exp2(sink - m)
        outs_w += _unstack_heads(_rows_to_tokens(oa / l), cn)
        st = _dot_nt(gk_ref[0, kv], _stack_heads(gq, kv))
        m = st.max(axis=0, keepdims=True)
        oa = _dot(gv_ref[0, kv], jnp.exp2(st - m).astype(BF16))
        outs_g += _unstack_heads(_rows_to_tokens(oa / oa[HEAD_DIM:HEAD_DIM + 1, :]), cn)
    ow_ref[0] = jnp.concatenate(outs_w, axis=1).astype(BF16)
    og_ref[0] = jnp.concatenate(outs_g, axis=1).astype(BF16)


def _context_attention(sink, wq, wk, wv, gq, gk, gv):
    bn, cn, aw = wq.shape
    kw = wk.shape[2]
    n_kv = gk.shape[1]
    tok = lambda w: pl.BlockSpec((1, cn, w), lambda b: (b, 0, 0))
    hd = lambda r, w: pl.BlockSpec((1, n_kv, r, w), lambda b: (b, 0, 0, 0))
    shp = jax.ShapeDtypeStruct((bn, cn, aw), BF16)
    return pl.pallas_call(
        _ctx_attn_kernel,
        out_shape=(shp, shp),
        grid=(bn,),
        in_specs=[pl.BlockSpec(memory_space=pltpu.SMEM), tok(aw), tok(kw), hd(VT_ROWS, cn), tok(aw),
                  hd(cn, HEAD_DIM), hd(VT_ROWS, cn)],
        out_specs=(tok(aw), tok(aw)),
        compiler_params=_cparams(1),
        name="context_attention",
    )(sink, wq, wk, wv, gq, gk, gv)


def _merge_kernel(x_ref, h_ref, mod_ref, g2_ref, u_ref, up_ref, un_ref, wp_ref, ps_ref, yw_ref, yg_ref,
                  wg_ref, bg_ref, wb_ref, wo_ref, x1_ref, h2_ref, *, ln):
    x = x_ref[0]
    h = h_ref[0]
    ys = ((1, yw_ref[0]), (2, yg_ref[0]), (0, _pool_tile(u_ref, up_ref, un_ref, wp_ref, ps_ref, ln)))
    merged = None
    for i, y in ys:
        gate = jax.nn.sigmoid(_dot(h, wg_ref[i]) + bg_ref[i:i + 1, :])
        term = gate * _dot(y, wb_ref[i])
        merged = term if merged is None else merged + term
    x1 = x + mod_ref[0, 2:3, :] * _dot(merged.astype(BF16), wo_ref[...])
    x1_ref[0] = x1
    h2_ref[0] = _rms_mod(x1, g2_ref[...], mod_ref[0, 3:4, :], mod_ref[0, 4:5, :]).astype(BF16)


def _merge(x, h, mod, g2, u, w_grp, p_scale, yw, yg, wg, bg, wb, wo, *, tl):
    bn, ln, d = x.shape
    bw = yw.shape[2]
    r = tl // POOL_REACH
    n_halo = ln // POOL_REACH
    halo = lambda f: pl.BlockSpec((1, POOL_REACH, bw), f)
    mod_map = (lambda b, i: (b, 0, 0)) if mod.shape[0] > 1 else (lambda b, i: (0, 0, 0))
    tok = lambda w: pl.BlockSpec((1, tl, w), lambda b, i: (b, i, 0))
    full = _resident
    return pl.pallas_call(
        functools.partial(_merge_kernel, ln=ln),
        out_shape=(jax.ShapeDtypeStruct((bn, ln, d), F32), jax.ShapeDtypeStruct((bn, ln, d), BF16)),
        grid=(bn, ln // tl),
        in_specs=[tok(d), tok(d), pl.BlockSpec((1, 6, d), mod_map), full(g2),
                  tok(bw),
                  halo(lambda b, i: (b, jnp.maximum(i * r - 1, 0), 0)),
                  halo(lambda b, i: (b, jnp.minimum((i + 1) * r, n_halo - 1), 0)),
                  full(w_grp), full(p_scale), tok(bw), tok(bw),
                  full(wg), full(bg), full(wb), full(wo)],
        out_specs=(tok(d), tok(d)),
        compiler_params=_cparams(2),
        name="merge_branches",
    )(x, h, mod, g2, u, u, u, w_grp, p_scale, yw, yg, wg, bg, wb, wo)


def _ffn_kernel(h_ref, hp_ref, hn_ref, x_ref, mod_ref, wg_ref, wv_ref, cw_ref, cb_ref, wd_ref, fg_ref,
                o_ref, g_scr, *, final):
    j = pl.program_id(1)
    nj = pl.num_programs(1)
    tl = h_ref.shape[1]
    halo = hp_ref.shape[1]
    hm = h_ref[0]
    hp = jnp.where(j > 0, hp_ref[0], jnp.zeros_like(hp_ref[0]))
    hn = jnp.where(j < nj - 1, hn_ref[0], jnp.zeros_like(hn_ref[0]))
    hext = jnp.concatenate([hp, hm, hn], axis=0)
    g_scr[...] = _dot(hext, wg_ref[...])
    a = (g_scr[pl.ds(halo - 1, tl), :] * cw_ref[0:1, :]
         + g_scr[pl.ds(halo, tl), :] * cw_ref[1:2, :]
         + g_scr[pl.ds(halo + 1, tl), :] * cw_ref[2:3, :]
         + cb_ref[...])
    val = _dot(hm, wv_ref[...])
    act = (a * jax.nn.sigmoid(a)) * val
    x2 = x_ref[0] + mod_ref[0, 5:6, :] * _dot(act.astype(BF16), wd_ref[...])
    if final:
        x2 = x2 * lax.rsqrt(jnp.mean(x2 * x2, axis=-1, keepdims=True) + EPS) * fg_ref[...]
    o_ref[0] = x2


def _ffn(h2, x1, mod, wg, wv, cw, cb, wd, fg, *, tl, final):
    bn, ln, d = x1.shape
    f = wg.shape[1]
    halo = BF16_SUBLANES
    r = tl // halo
    nh = ln // halo
    mod_map = (lambda b, i: (b, 0, 0)) if mod.shape[0] > 1 else (lambda b, i: (0, 0, 0))
    tok = pl.BlockSpec((1, tl, d), lambda b, i: (b, i, 0))
    full = _resident
    return pl.pallas_call(
        functools.partial(_ffn_kernel, final=final),
        out_shape=jax.ShapeDtypeStruct((bn, ln, d), F32),
        grid=(bn, ln // tl),
        in_specs=[
            tok,
            pl.BlockSpec((1, halo, d), lambda b, i: (b, jnp.maximum(i * r - 1, 0), 0)),
            pl.BlockSpec((1, halo, d), lambda b, i: (b, jnp.minimum((i + 1) * r, nh - 1), 0)),
            tok,
            pl.BlockSpec((1, 6, d), mod_map),
            full(wg), full(wv), full(cw), full(cb), full(wd), full(fg),
        ],
        out_specs=tok,
        scratch_shapes=[pltpu.VMEM((tl + 2 * halo, f), F32)],
        compiler_params=_cparams(2),
        name="conv_glu_final" if final else "conv_glu",
    )(h2, h2, h2, x1, mod, wg, wv, cw, cb, wd, fg)


def _rope_tables(n_tok):
    pos = jnp.arange(n_tok)
    row = (pos // GRID_W).astype(F32)
    col = (pos % GRID_W).astype(F32)
    half = HEAD_DIM // 2
    inv = ROPE_THETA ** (-jnp.arange(0, half, 2, dtype=F32) / half)
    ang_r = row[:, None] * inv
    ang_c = col[:, None] * inv
    cos64 = jnp.concatenate([jnp.cos(ang_r)] * 2 + [jnp.cos(ang_c)] * 2, axis=1)
    sin64 = jnp.concatenate([-jnp.sin(ang_r), jnp.sin(ang_r), -jnp.sin(ang_c), jnp.sin(ang_c)], axis=1)
    reps = LANES // HEAD_DIM
    return jnp.tile(cos64, (1, reps)), jnp.tile(sin64, (1, reps))


def _tile_rows(n, target):
    t = min(n, target)
    while n % t:
        t //= 2
    return t


def kernel(x, c, ctx, c_ctx, w_mod, b_mod, norm1_g, norm2_g, w_in, w_pool_grp, pool_scale, win_sink,
           q_norm_g, k_norm_g, w_branch, w_gate, b_gate, w_out, w_ff_gate, w_ff_val, conv_w, conv_b,
           w_ff_down, final_g):
    bn, ln, d = x.shape
    cn = ctx.shape[1]
    depth = w_mod.shape[0]
    attn_w = d // 2
    kv_w = attn_w // Q_PER_KV

    mod_all = _modulation(jnp.concatenate([c, c_ctx[None]], axis=0), w_mod, b_mod)
    mod_all = mod_all.reshape(depth, bn + 1, 6, d)

    cos, sin = _rope_tables(ln)
    cos_c = jnp.ones((cn, LANES), F32)
    sin_c = jnp.zeros((cn, LANES), F32)
    ids = np.arange(2 * LANES) // HEAD_DIM
    bd = jnp.asarray(ids[:, None] == ids[None, :], dtype=BF16)
    fg = final_g.reshape(1, d)

    tl = _tile_rows(ln, DENSE_TILE)
    tl_ffn = _tile_rows(ln, FFN_TILE)
    tl_c = _tile_rows(cn, CTX_TILE)
    tq = _tile_rows(ln, GLOBAL_QUERY_TILE)
    win_qb = _tile_rows(ln // BLOCK, WINDOW_BLOCKS_PER_STEP)

    for l in range(depth):
        last = l == depth - 1
        mod_l = mod_all[l, :bn]
        mod_c = mod_all[l, bn:]
        g1 = norm1_g[l].reshape(1, d)
        g2 = norm2_g[l].reshape(1, d)
        w_in_l = w_in[l].astype(BF16)
        qg = jnp.tile(q_norm_g[l], attn_w // HEAD_DIM).reshape(1, attn_w)
        kg = jnp.tile(k_norm_g[l], kv_w // HEAD_DIM).reshape(1, kv_w)
        w_grp = w_pool_grp[l].astype(BF16)
        p_scale = pool_scale[l].reshape(1, -1)
        sink = win_sink[l]
        wg = w_gate[l].astype(BF16)
        bg = b_gate[l]
        wb = w_branch[l].astype(BF16)
        wo = w_out[l].astype(BF16)
        wfg = w_ff_gate[l].astype(BF16)
        wfv = w_ff_val[l].astype(BF16)
        wfd = w_ff_down[l].astype(BF16)
        cw = conv_w[l]
        cb = conv_b[l].reshape(1, -1)

        h_l, u_l, wq_l, wk_l, wv_l, gq_l, gk_l, gv_l = _inproj(
            x, mod_l, g1, w_in_l, cos, sin, qg, kg, bd, rope=True, tl=tl)
        h_c, u_c, wq_c, wk_c, wv_c, gq_c, gk_c, gv_c = _inproj(
            ctx, mod_c, g1, w_in_l, cos_c, sin_c, qg, kg, bd, rope=False, tl=tl_c)

        y_win = _window_attention(sink, wq_l, wk_l, wv_l, wk_c, wv_c, qb=win_qb)
        bound = _score_bound(q_norm_g[l], k_norm_g[l])
        y_glob = _global_attention(bound, gq_l, gk_l, gv_l, gk_c, gv_c, tq=tq, kc=GLOBAL_KEY_CHUNK)
        x1, h2 = _merge(x, h_l, mod_l, g2, u_l, w_grp, p_scale, y_win, y_glob, wg, bg, wb, wo, tl=tl)

        if not last:
            yc_win, yc_glob = _context_attention(sink, wq_c, wk_c, wv_c, gq_c, gk_c, gv_c)
            c1, hc2 = _merge(ctx, h_c, mod_c, g2, u_c, w_grp, p_scale, yc_win, yc_glob, wg, bg, wb, wo, tl=tl_c)
            ctx = _ffn(hc2, c1, mod_c, wfg, wfv, cw, cb, wfd, fg, tl=tl_c, final=False)

        x = _ffn(h2, x1, mod_l, wfg, wfv, cw, cb, wfd, fg, tl=tl_ffn, final=last)
    return x
```

```python
import functools

import jax
import jax.numpy as jnp
import numpy as np
from jax import lax
from jax.experimental import pallas as pl
from jax.experimental.pallas import tpu as pltpu

F32 = jnp.float32
BF16 = jnp.bfloat16

GRID_W = 64
HEAD_DIM = 64
Q_PER_KV = 4
POOL_WINDOWS = (2, 4, 8, 16)
POOL_REACH = max(POOL_WINDOWS) // 2
WINDOW = 128
BLOCK = 128
ROPE_THETA = 10000.0
EPS = 1e-6
NEG = -1e30
SM_SCALE = HEAD_DIM ** -0.5
LOG2E = 1.4426950408889634
Q_SCALE = SM_SCALE * LOG2E
BOUND_SLACK = 1.02
BOUND_SHIFT = 60.0
BOUND_MAX = 80.0
LANES = 128
VT_ROWS = 128
BF16_SUBLANES = 16
VMEM_LIMIT = 56 * 1024 * 1024

DENSE_TILE = 1024
FFN_TILE = 512
CTX_TILE = 256
GLOBAL_QUERY_TILE = 512
GLOBAL_KEY_CHUNK = 1024
WINDOW_BLOCKS_PER_STEP = 16
MOD_COLS = 1536


def _cparams(n_axes):
    return pltpu.CompilerParams(
        dimension_semantics=("arbitrary",) * n_axes,
        vmem_limit_bytes=VMEM_LIMIT,
    )


def _resident(a):
    return pl.BlockSpec(a.shape, lambda *_: (0,) * a.ndim, pipeline_mode=pl.Buffered(1))


def _rms_mod(x, g, shift, scale):
    y = x * lax.rsqrt(jnp.mean(x * x, axis=-1, keepdims=True) + EPS)
    return (y * g) * (1.0 + scale) + shift


def _dot(a, b):
    return jnp.dot(a, b, preferred_element_type=F32)


def _dot_nt(a, b):
    return lax.dot_general(a, b, (((1,), (1,)), ((), ())), preferred_element_type=F32)


def _mod_kernel(c_ref, w_ref, b_ref, o_ref):
    c = c_ref[...]
    s = c * jax.nn.sigmoid(c)
    o_ref[0] = _dot(s.astype(BF16), w_ref[0].astype(BF16)) + b_ref[0]


def _modulation(cc, w_mod, b_mod):
    depth, d, n = w_mod.shape
    rows = cc.shape[0]
    tn = _tile_rows(n, MOD_COLS)
    return pl.pallas_call(
        _mod_kernel,
        out_shape=jax.ShapeDtypeStruct((depth, rows, n), F32),
        grid=(depth, n // tn),
        in_specs=[
            pl.BlockSpec((rows, d), lambda l, j: (0, 0)),
            pl.BlockSpec((1, d, tn), lambda l, j: (l, 0, j)),
            pl.BlockSpec((1, 1, tn), lambda l, j: (l, 0, j)),
        ],
        out_specs=pl.BlockSpec((1, rows, tn), lambda l, j: (l, 0, j)),
        compiler_params=_cparams(2),
        name="modulation",
    )(cc, w_mod, b_mod.reshape(depth, 1, n))


def _head_rms(t, bd, g):
    sq = t * t
    hi = sq.astype(BF16)
    lo = (sq - hi.astype(F32)).astype(BF16)
    w = t.shape[1]
    parts = []
    for j in range(0, w, 2 * LANES):
        e = min(j + 2 * LANES, w)
        b = bd[: e - j, : e - j]
        parts.append(_dot(hi[:, j:e], b) + _dot(lo[:, j:e], b))
    ssq = parts[0] if len(parts) == 1 else jnp.concatenate(parts, axis=1)
    return t * lax.rsqrt(ssq * (1.0 / HEAD_DIM) + EPS) * g


def _rope(t, cos, sin, partner_lane):
    outs = []
    for j in range(0, t.shape[1], LANES):
        tj = t[:, j:j + LANES]
        partner = jnp.take_along_axis(tj, partner_lane, axis=1)
        outs.append(tj * cos + partner * sin)
    return outs[0] if len(outs) == 1 else jnp.concatenate(outs, axis=1)


def _inproj_kernel(x_ref, mod_ref, g1_ref, w_ref, cos_ref, sin_ref, qg_ref, kg_ref, bd_ref,
                   h_ref, u_ref, wq_ref, wk_ref, wv_ref, gq_ref, gk_ref, gv_ref, *, rope, pool_w, attn_w, kv_w):
    x = x_ref[0]
    h = _rms_mod(x, g1_ref[...], mod_ref[0, 0:1, :], mod_ref[0, 1:2, :])
    h = h.astype(BF16)
    h_ref[0] = h
    mix_w = attn_w + 2 * kv_w
    zg = _dot(h, w_ref[:, pool_w + mix_w:pool_w + 2 * mix_w])
    zw = _dot(h, w_ref[:, pool_w:pool_w + mix_w])
    u_ref[0] = _dot(h, w_ref[:, :pool_w])
    gq, gk, gv = zg[:, :attn_w], zg[:, attn_w:attn_w + kv_w], zg[:, attn_w + kv_w:]
    wq, wk, wv = zw[:, :attn_w], zw[:, attn_w:attn_w + kv_w], zw[:, attn_w + kv_w:]
    bd = bd_ref[...]
    gq = _head_rms(gq, bd, qg_ref[...])
    gk = _head_rms(gk, bd, kg_ref[...])
    if rope:
        cos = cos_ref[...]
        sin = sin_ref[...]
        lane = lax.broadcasted_iota(jnp.int32, cos.shape, 1)
        partner_lane = lane ^ 16
        wq = _rope(wq, cos, sin, partner_lane)
        wk = _rope(wk, cos, sin, partner_lane)
        gq = _rope(gq, cos, sin, partner_lane)
        gk = _rope(gk, cos, sin, partner_lane)
    wq_ref[0] = (wq * Q_SCALE).astype(BF16)
    gq_ref[0] = (gq * Q_SCALE).astype(BF16)
    wk_ref[0] = wk.astype(BF16)
    tail_row = lax.broadcasted_iota(jnp.int32, (VT_ROWS - HEAD_DIM, gv.shape[0]), 0)
    tail = jnp.where(tail_row == 0, 1.0, 0.0).astype(BF16)
    for v, v_ref in ((wv, wv_ref), (gv, gv_ref)):
        vt = v.T
        for k in range(kv_w // HEAD_DIM):
            v_ref[0, k, 0:HEAD_DIM, :] = vt[k * HEAD_DIM:(k + 1) * HEAD_DIM].astype(BF16)
            v_ref[0, k, HEAD_DIM:VT_ROWS, :] = tail
    for k in range(kv_w // HEAD_DIM):
        gk_ref[0, k] = gk[:, k * HEAD_DIM:(k + 1) * HEAD_DIM].astype(BF16)


def _inproj(x, mod, g1, w_in, cos, sin, qg, kg, bd, *, rope, tl):
    bn, ln, d = x.shape
    in_w = w_in.shape[1]
    attn_w = qg.shape[1]
    kv_w = kg.shape[1]
    pool_w = in_w - 2 * attn_w - 4 * kv_w
    n_kv = kv_w // HEAD_DIM
    mod_b = mod.shape[0]
    mod_map = (lambda b, i: (b, 0, 0)) if mod_b > 1 else (lambda b, i: (0, 0, 0))
    tok = lambda w: pl.BlockSpec((1, tl, w), lambda b, i: (b, i, 0))
    full2 = _resident
    kern = functools.partial(_inproj_kernel, rope=rope, pool_w=pool_w, attn_w=attn_w, kv_w=kv_w)
    out_shapes = (
        jax.ShapeDtypeStruct((bn, ln, d), BF16),
        jax.ShapeDtypeStruct((bn, ln, pool_w), F32),
        jax.ShapeDtypeStruct((bn, ln, attn_w), BF16),
        jax.ShapeDtypeStruct((bn, ln, kv_w), BF16),
        jax.ShapeDtypeStruct((bn, n_kv, VT_ROWS, ln), BF16),
        jax.ShapeDtypeStruct((bn, ln, attn_w), BF16),
        jax.ShapeDtypeStruct((bn, n_kv, ln, HEAD_DIM), BF16),
        jax.ShapeDtypeStruct((bn, n_kv, VT_ROWS, ln), BF16),
    )
    assert kv_w == LANES
    head_spec = pl.BlockSpec((1, n_kv, tl, HEAD_DIM), lambda b, i: (b, 0, i, 0))
    vt_spec = pl.BlockSpec((1, n_kv, VT_ROWS, tl), lambda b, i: (b, 0, 0, i))
    return pl.pallas_call(
        kern,
        out_shape=out_shapes,
        grid=(bn, ln // tl),
        in_specs=[
            tok(d),
            pl.BlockSpec((1, 6, d), mod_map),
            full2(g1),
            full2(w_in),
            pl.BlockSpec((tl, LANES), lambda b, i: (i, 0)),
            pl.BlockSpec((tl, LANES), lambda b, i: (i, 0)),
            full2(qg),
            full2(kg),
            full2(bd),
        ],
        out_specs=(tok(d), tok(pool_w), tok(attn_w), tok(kv_w), vt_spec, tok(attn_w), head_spec, vt_spec),
        compiler_params=_cparams(2),
        name="inproj_rope" if rope else "inproj_ctx",
    )(x, mod, g1, w_in, cos, sin, qg, kg, bd)


def _pool_tile(u_ref, up_ref, un_ref, w_ref, s_ref, ln):
    i = pl.program_id(1)
    ni = pl.num_programs(1)
    tl = u_ref.shape[1]
    n_grp = len(POOL_WINDOWS)
    gw = u_ref.shape[2] // n_grp
    ext_len = tl + 2 * POOL_REACH
    up = jnp.where(i > 0, up_ref[0], 0.0)
    un = jnp.where(i < ni - 1, un_ref[0], 0.0)
    t = i * tl + lax.broadcasted_iota(jnp.int32, (tl, 1), 0)
    outs = []
    for gi, win in enumerate(POOL_WINDOWS):
        sl = slice(gi * gw, (gi + 1) * gw)
        u = u_ref[0, :, sl]
        e = jnp.concatenate([up[:, sl], u, un[:, sl]], axis=0)
        half = win // 2
        fwd = e
        step = 1
        while step < half:
            fwd = fwd + pltpu.roll(fwd, ext_len - step, axis=0)
            step *= 2
        wsum = (fwd + pltpu.roll(fwd, half, axis=0))[POOL_REACH:POOL_REACH + tl, :]
        cnt = (jnp.minimum(t + win // 2, ln) - jnp.maximum(t - win // 2, 0)).astype(F32)
        p = wsum / cnt - u
        outs.append((_dot(p.astype(BF16), w_ref[gi]) * s_ref[:, sl]).astype(BF16))
    return jnp.concatenate(outs, axis=1)


def _stack_heads(q, kv):
    return jnp.concatenate(
        [q[:, (kv * Q_PER_KV + g) * HEAD_DIM:(kv * Q_PER_KV + g + 1) * HEAD_DIM] for g in range(Q_PER_KV)], axis=0)


def _rows_to_tokens(ot):
    pad = LANES - ot.shape[0]
    if pad:
        ot = jnp.concatenate([ot, jnp.zeros((pad, ot.shape[1]), F32)], axis=0)
    return ot.T[:, :HEAD_DIM]


def _unstack_heads(o, t):
    return [o[g * t:(g + 1) * t] for g in range(Q_PER_KV)]


def _sink_row(sink_ref, kv, t):
    lane = lax.broadcasted_iota(jnp.int32, (1, Q_PER_KV * t), 1)
    row = jnp.full((1, Q_PER_KV * t), sink_ref[kv * Q_PER_KV + Q_PER_KV - 1] * LOG2E, F32)
    for g in range(Q_PER_KV - 2, -1, -1):
        row = jnp.where(lane < (g + 1) * t, sink_ref[kv * Q_PER_KV + g] * LOG2E, row)
    return row


def _win_kernel(sink_ref, q_ref, kp_ref, kc_ref, kn_ref, vp_ref, vc_ref, vn_ref, kx_ref, vx_ref, o_ref,
                pl_ref, px_ref, es_ref, *, qb, ni):
    t = pl.program_id(0)
    i = jnp.minimum(t, pl.num_programs(0) - 2) % ni
    kall = jnp.concatenate([kp_ref[0], kc_ref[0], kn_ref[0]], axis=0)
    vall = jnp.concatenate([vp_ref[0], vc_ref[0], vn_ref[0]], axis=2)
    kx = kx_ref[0]
    n_kv = kall.shape[1] // HEAD_DIM
    key = lax.broadcasted_iota(jnp.int32, (3 * BLOCK, BLOCK), 0)
    qry = lax.broadcasted_iota(jnp.int32, (3 * BLOCK, BLOCK), 1)
    band = jnp.where(jnp.abs(key - BLOCK - qry) <= WINDOW, 0.0, NEG)
    no_prev = jnp.where((key < BLOCK) & (i == 0), NEG, 0.0)
    no_next = jnp.where((key >= 2 * BLOCK) & (i == ni - 1), NEG, 0.0)

    @pl.when(t == 0)
    def _():
        pl_ref[...] = jnp.zeros_like(pl_ref)
        px_ref[...] = jnp.zeros_like(px_ref)
        es_ref[...] = jnp.ones_like(es_ref)

    blocks = []
    for j in range(qb):
        bias = band
        if j == 0:
            bias = bias + no_prev
        if j == qb - 1:
            bias = bias + no_next
        bias = jnp.concatenate([bias] * Q_PER_KV, axis=1)
        q = q_ref[0, j * BLOCK:(j + 1) * BLOCK, :]
        kwin = kall[j * BLOCK:(j + 3) * BLOCK]
        outs = []
        for kv in range(n_kv):
            g = j * n_kv + kv
            sl = slice(kv * HEAD_DIM, (kv + 1) * HEAD_DIM)
            oa = (_dot(vall[kv, :, j * BLOCK:(j + 3) * BLOCK], pl_ref[g])
                  + _dot(vx_ref[0, kv], px_ref[g]))
            l = oa[HEAD_DIM:HEAD_DIM + 1, :] + es_ref[g]
            outs += _unstack_heads(_rows_to_tokens(oa / l), BLOCK)
            q4 = _stack_heads(q, kv)
            s_lat = _dot_nt(kwin[:, sl], q4) + bias
            s_ctx = _dot_nt(kx[:, sl], q4)
            sink = _sink_row(sink_ref, kv, BLOCK)
            m = jnp.maximum(jnp.maximum(s_lat.max(axis=0, keepdims=True), s_ctx.max(axis=0, keepdims=True)),
                            sink)
            pl_ref[g] = jnp.exp2(s_lat - m).astype(BF16)
            px_ref[g] = jnp.exp2(s_ctx - m).astype(BF16)
            es_ref[g] = jnp.exp2(sink - m)
        blocks.append(jnp.concatenate(outs, axis=1).astype(BF16))
    o_ref[0] = blocks[0] if qb == 1 else jnp.concatenate(blocks, axis=0)


def _window_attention(sink, q, k, vt, kx, vtx, *, qb):
    bn, ln, aw = q.shape
    kw = k.shape[2]
    cn = kx.shape[1]
    n_kv = kw // HEAD_DIM
    nb = ln // BLOCK
    assert kw == LANES and nb % qb == 0
    ni = nb // qb
    n_tiles = bn * ni
    n_grp = qb * n_kv
    mq = Q_PER_KV * BLOCK

    def qk_tile(t):
        t = jnp.minimum(t, n_tiles - 1)
        return t // ni, t % ni

    def pv_tile(t):
        t = jnp.maximum(t - 1, 0)
        return t // ni, t % ni

    def blocks_of(tile):
        cur = lambda t: tile(t)[1]
        prev = lambda t: jnp.maximum(tile(t)[1] * qb - 1, 0)
        nxt = lambda t: jnp.minimum((tile(t)[1] + 1) * qb, nb - 1)
        return cur, prev, nxt

    k_cur, k_prev, k_nxt = blocks_of(qk_tile)
    v_cur, v_prev, v_nxt = blocks_of(pv_tile)
    kspec = lambda rows, f: pl.BlockSpec((1, rows, kw), lambda t: (qk_tile(t)[0], f(t), 0))
    vspec = lambda cols, f: pl.BlockSpec((1, n_kv, VT_ROWS, cols), lambda t: (pv_tile(t)[0], 0, 0, f(t)))
    zero = lambda t: 0
    return pl.pallas_call(
        functools.partial(_win_kernel, qb=qb, ni=ni),
        out_shape=jax.ShapeDtypeStruct((bn, ln, aw), BF16),
        grid=(n_tiles + 1,),
        in_specs=[
            pl.BlockSpec(memory_space=pltpu.SMEM),
            pl.BlockSpec((1, qb * BLOCK, aw), lambda t: (qk_tile(t)[0], k_cur(t), 0)),
            kspec(BLOCK, k_prev), kspec(qb * BLOCK, k_cur), kspec(BLOCK, k_nxt),
            vspec(BLOCK, v_prev), vspec(qb * BLOCK, v_cur), vspec(BLOCK, v_nxt),
            kspec(cn, zero), vspec(cn, zero),
        ],
        out_specs=pl.BlockSpec((1, qb * BLOCK, aw), lambda t: (pv_tile(t)[0], v_cur(t), 0)),
        scratch_shapes=[
            pltpu.VMEM((n_grp, 3 * BLOCK, mq), BF16),
            pltpu.VMEM((n_grp, cn, mq), BF16),
            pltpu.VMEM((n_grp, 1, mq), F32),
        ],
        compiler_params=_cparams(1),
        name="window_attention",
    )(sink, q, k, k, k, vt, vt, vt, kx, vtx)


def _glob_kernel(bound_ref, q_ref, k_ref, v_ref, kx_ref, vx_ref, o_ref, m_ref, p_ref, *, kc):
    s = pl.program_id(0)
    ln = k_ref.shape[2]
    cn = kx_ref.shape[2]
    q = q_ref[0]
    tq = q.shape[0]
    q4 = _stack_heads(q, 0)
    pieces = [(k_ref, v_ref, c0, min(kc, ln - c0), c0) for c0 in range(0, ln, kc)] + [(kx_ref, vx_ref, 0, cn, ln)]

    @pl.when(s == 0)
    def _():
        p_ref[...] = jnp.zeros_like(p_ref)

    bound = bound_ref[0]
    m_ref[...] = jnp.full(m_ref.shape, bound - BOUND_SHIFT, F32)

    @pl.when(bound > BOUND_MAX)
    def _():
        m = None
        for kr, _, c0, n, _ in pieces:
            mc = _dot_nt(kr[0, 0, c0:c0 + n, :], q4).max(axis=0, keepdims=True)
            m = mc if m is None else jnp.maximum(m, mc)
        m_ref[...] = m

    m = m_ref[...]
    acc = None
    for kr, vr, c0, n, off in pieces:
        oc = _dot(vr[0, 0, :, c0:c0 + n], p_ref[off:off + n, :])
        acc = oc if acc is None else acc + oc
        p_ref[off:off + n, :] = jnp.exp2(_dot_nt(kr[0, 0, c0:c0 + n, :], q4) - m).astype(BF16)
    l = acc[HEAD_DIM:HEAD_DIM + 1, :]
    l = jnp.where(l > 0.0, l, 1.0)
    o = _rows_to_tokens(acc / l)
    o_ref[0] = jnp.concatenate(_unstack_heads(o, tq), axis=1).astype(BF16)


def _score_bound(qg, kg):
    return (HEAD_DIM * Q_SCALE * BOUND_SLACK * jnp.max(jnp.abs(qg)) * jnp.max(jnp.abs(kg))).reshape(1)


def _global_attention(bound, q, k, v, kx, vx, *, tq, kc):
    bn, ln, aw = q.shape
    n_kv = k.shape[1]
    cn = kx.shape[2]
    gw = aw // n_kv
    assert gw == Q_PER_KV * HEAD_DIM
    nq = ln // tq
    n_tiles = bn * n_kv * nq

    def split(t):
        return t // (n_kv * nq), (t // nq) % n_kv, t % nq

    def qk_tile(s):
        return split(jnp.minimum(s, n_tiles - 1))

    def pv_tile(s):
        return split(jnp.maximum(s - 1, 0))

    def q_map(s):
        b, h, i = qk_tile(s)
        return b, i, h

    def k_map(s):
        b, h, _ = qk_tile(s)
        return b, h, 0, 0

    def v_map(s):
        b, h, _ = pv_tile(s)
        return b, h, 0, 0

    def o_map(s):
        b, h, i = pv_tile(s)
        return b, i, h

    return pl.pallas_call(
        functools.partial(_glob_kernel, kc=kc),
        out_shape=jax.ShapeDtypeStruct((bn, ln, aw), BF16),
        grid=(n_tiles + 1,),
        in_specs=[
            pl.BlockSpec(memory_space=pltpu.SMEM),
            pl.BlockSpec((1, tq, gw), q_map),
            pl.BlockSpec((1, 1, ln, HEAD_DIM), k_map),
            pl.BlockSpec((1, 1, VT_ROWS, ln), v_map),
            pl.BlockSpec((1, 1, cn, HEAD_DIM), k_map),
            pl.BlockSpec((1, 1, VT_ROWS, cn), v_map),
        ],
        out_specs=pl.BlockSpec((1, tq, gw), o_map),
        scratch_shapes=[pltpu.VMEM((1, Q_PER_KV * tq), F32), pltpu.VMEM((ln + cn, Q_PER_KV * tq), BF16)],
        compiler_params=_cparams(1),
        name="global_attention",
    )(bound, q, k, v, kx, vx)


def _ctx_attn_kernel(sink_ref, wq_ref, wk_ref, wv_ref, gq_ref, gk_ref, gv_ref, ow_ref, og_ref):
    wq = wq_ref[0]
    gq = gq_ref[0]
    wk = wk_ref[0]
    cn, kw = wk.shape
    outs_w = []
    outs_g = []
    for kv in range(kw // HEAD_DIM):
        st = _dot_nt(wk[:, kv * HEAD_DIM:(kv + 1) * HEAD_DIM], _stack_heads(wq, kv))
        sink = _sink_row(sink_ref, kv, cn)
        m = jnp.maximum(st.max(axis=0, keepdims=True), sink)
        oa = _dot(wv_ref[0, kv], jnp.exp2(st - m).astype(BF16))
        l = oa[HEAD_DIM:HEAD_DIM + 1, :] + jnp.exp2(sink - m)
        outs_w += _unstack_heads(_rows_to_tokens(oa / l), cn)
        st = _dot_nt(gk_ref[0, kv], _stack_heads(gq, kv))
        m = st.max(axis=0, keepdims=True)
        oa = _dot(gv_ref[0, kv], jnp.exp2(st - m).astype(BF16))
        outs_g += _unstack_heads(_rows_to_tokens(oa / oa[HEAD_DIM:HEAD_DIM + 1, :]), cn)
    ow_ref[0] = jnp.concatenate(outs_w, axis=1).astype(BF16)
    og_ref[0] = jnp.concatenate(outs_g, axis=1).astype(BF16)


def _context_attention(sink, wq, wk, wv, gq, gk, gv):
    bn, cn, aw = wq.shape
    kw = wk.shape[2]
    n_kv = gk.shape[1]
    tok = lambda w: pl.BlockSpec((1, cn, w), lambda b: (b, 0, 0))
    hd = lambda r, w: pl.BlockSpec((1, n_kv, r, w), lambda b: (b, 0, 0, 0))
    shp = jax.ShapeDtypeStruct((bn, cn, aw), BF16)
    return pl.pallas_call(
        _ctx_attn_kernel,
        out_shape=(shp, shp),
        grid=(bn,),
        in_specs=[pl.BlockSpec(memory_space=pltpu.SMEM), tok(aw), tok(kw), hd(VT_ROWS, cn), tok(aw),
                  hd(cn, HEAD_DIM), hd(VT_ROWS, cn)],
        out_specs=(tok(aw), tok(aw)),
        compiler_params=_cparams(1),
        name="context_attention",
    )(sink, wq, wk, wv, gq, gk, gv)


def _merge_kernel(x_ref, h_ref, mod_ref, g2_ref, u_ref, up_ref, un_ref, wp_ref, ps_ref, yw_ref, yg_ref,
                  wg_ref, bg_ref, wb_ref, wo_ref, x1_ref, h2_ref, *, ln):
    x = x_ref[0]
    h = h_ref[0]
    ys = ((1, yw_ref[0]), (2, yg_ref[0]), (0, _pool_tile(u_ref, up_ref, un_ref, wp_ref, ps_ref, ln)))
    merged = None
    for i, y in ys:
        gate = jax.nn.sigmoid(_dot(h, wg_ref[i]) + bg_ref[i:i + 1, :])
        term = gate * _dot(y, wb_ref[i])
        merged = term if merged is None else merged + term
    x1 = x + mod_ref[0, 2:3, :] * _dot(merged.astype(BF16), wo_ref[...])
    x1_ref[0] = x1
    h2_ref[0] = _rms_mod(x1, g2_ref[...], mod_ref[0, 3:4, :], mod_ref[0, 4:5, :]).astype(BF16)


def _merge(x, h, mod, g2, u, w_grp, p_scale, yw, yg, wg, bg, wb, wo, *, tl):
    bn, ln, d = x.shape
    bw = yw.shape[2]
    r = tl // POOL_REACH
    n_halo = ln // POOL_REACH
    halo = lambda f: pl.BlockSpec((1, POOL_REACH, bw), f)
    mod_map = (lambda b, i: (b, 0, 0)) if mod.shape[0] > 1 else (lambda b, i: (0, 0, 0))
    tok = lambda w: pl.BlockSpec((1, tl, w), lambda b, i: (b, i, 0))
    full = _resident
    return pl.pallas_call(
        functools.partial(_merge_kernel, ln=ln),
        out_shape=(jax.ShapeDtypeStruct((bn, ln, d), F32), jax.ShapeDtypeStruct((bn, ln, d), BF16)),
        grid=(bn, ln // tl),
        in_specs=[tok(d), tok(d), pl.BlockSpec((1, 6, d), mod_map), full(g2),
                  tok(bw),
                  halo(lambda b, i: (b, jnp.maximum(i * r - 1, 0), 0)),
                  halo(lambda b, i: (b, jnp.minimum((i + 1) * r, n_halo - 1), 0)),
                  full(w_grp), full(p_scale), tok(bw), tok(bw),
                  full(wg), full(bg), full(wb), full(wo)],
        out_specs=(tok(d), tok(d)),
        compiler_params=_cparams(2),
        name="merge_branches",
    )(x, h, mod, g2, u, u, u, w_grp, p_scale, yw, yg, wg, bg, wb, wo)


def _ffn_kernel(h_ref, hp_ref, hn_ref, x_ref, mod_ref, wg_ref, wv_ref, cw_ref, cb_ref, wd_ref, fg_ref,
                o_ref, g_scr, *, final):
    j = pl.program_id(1)
    nj = pl.num_programs(1)
    tl = h_ref.shape[1]
    halo = hp_ref.shape[1]
    hm = h_ref[0]
    hp = jnp.where(j > 0, hp_ref[0], jnp.zeros_like(hp_ref[0]))
    hn = jnp.where(j < nj - 1, hn_ref[0], jnp.zeros_like(hn_ref[0]))
    hext = jnp.concatenate([hp, hm, hn], axis=0)
    g_scr[...] = _dot(hext, wg_ref[...])
    a = (g_scr[pl.ds(halo - 1, tl), :] * cw_ref[0:1, :]
         + g_scr[pl.ds(halo, tl), :] * cw_ref[1:2, :]
         + g_scr[pl.ds(halo + 1, tl), :] * cw_ref[2:3, :]
         + cb_ref[...])
    val = _dot(hm, wv_ref[...])
    act = (a * jax.nn.sigmoid(a)) * val
    x2 = x_ref[0] + mod_ref[0, 5:6, :] * _dot(act.astype(BF16), wd_ref[...])
    if final:
        x2 = x2 * lax.rsqrt(jnp.mean(x2 * x2, axis=-1, keepdims=True) + EPS) * fg_ref[...]
    o_ref[0] = x2


def _ffn(h2, x1, mod, wg, wv, cw, cb, wd, fg, *, tl, final):
    bn, ln, d = x1.shape
    f = wg.shape[1]
    halo = BF16_SUBLANES
    r = tl // halo
    nh = ln // halo
    mod_map = (lambda b, i: (b, 0, 0)) if mod.shape[0] > 1 else (lambda b, i: (0, 0, 0))
    tok = pl.BlockSpec((1, tl, d), lambda b, i: (b, i, 0))
    full = _resident
    return pl.pallas_call(
        functools.partial(_ffn_kernel, final=final),
        out_shape=jax.ShapeDtypeStruct((bn, ln, d), F32),
        grid=(bn, ln // tl),
        in_specs=[
            tok,
            pl.BlockSpec((1, halo, d), lambda b, i: (b, jnp.maximum(i * r - 1, 0), 0)),
            pl.BlockSpec((1, halo, d), lambda b, i: (b, jnp.minimum((i + 1) * r, nh - 1), 0)),
            tok,
            pl.BlockSpec((1, 6, d), mod_map),
            full(wg), full(wv), full(cw), full(cb), full(wd), full(fg),
        ],
        out_specs=tok,
        scratch_shapes=[pltpu.VMEM((tl + 2 * halo, f), F32)],
        compiler_params=_cparams(2),
        name="conv_glu_final" if final else "conv_glu",
    )(h2, h2, h2, x1, mod, wg, wv, cw, cb, wd, fg)


def _rope_tables(n_tok):
    pos = jnp.arange(n_tok)
    row = (pos // GRID_W).astype(F32)
    col = (pos % GRID_W).astype(F32)
    half = HEAD_DIM // 2
    inv = ROPE_THETA ** (-jnp.arange(0, half, 2, dtype=F32) / half)
    ang_r = row[:, None] * inv
    ang_c = col[:, None] * inv
    cos64 = jnp.concatenate([jnp.cos(ang_r)] * 2 + [jnp.cos(ang_c)] * 2, axis=1)
    sin64 = jnp.concatenate([-jnp.sin(ang_r), jnp.sin(ang_r), -jnp.sin(ang_c), jnp.sin(ang_c)], axis=1)
    reps = LANES // HEAD_DIM
    return jnp.tile(cos64, (1, reps)), jnp.tile(sin64, (1, reps))


def _tile_rows(n, target):
    t = min(n, target)
    while n % t:
        t //= 2
    return t


def kernel(x, c, ctx, c_ctx, w_mod, b_mod, norm1_g, norm2_g, w_in, w_pool_grp, pool_scale, win_sink,
           q_norm_g, k_norm_g, w_branch, w_gate, b_gate, w_out, w_ff_gate, w_ff_val, conv_w, conv_b,
           w_ff_down, final_g):
    bn, ln, d = x.shape
    cn = ctx.shape[1]
    depth = w_mod.shape[0]
    attn_w = d // 2
    kv_w = attn_w // Q_PER_KV

    mod_all = _modulation(jnp.concatenate([c, c_ctx[None]], axis=0), w_mod, b_mod)
    mod_all = mod_all.reshape(depth, bn + 1, 6, d)

    cos, sin = _rope_tables(ln)
    cos_c = jnp.ones((cn, LANES), F32)
    sin_c = jnp.zeros((cn, LANES), F32)
    ids = np.arange(2 * LANES) // HEAD_DIM
    bd = jnp.asarray(ids[:, None] == ids[None, :], dtype=BF16)
    fg = final_g.reshape(1, d)

    tl = _tile_rows(ln, DENSE_TILE)
    tl_ffn = _tile_rows(ln, FFN_TILE)
    tl_c = _tile_rows(cn, CTX_TILE)
    tq = _tile_rows(ln, GLOBAL_QUERY_TILE)
    win_qb = _tile_rows(ln // BLOCK, WINDOW_BLOCKS_PER_STEP)

    for l in range(depth):
        last = l == depth - 1
        mod_l = mod_all[l, :bn]
        mod_c = mod_all[l, bn:]
        g1 = norm1_g[l].reshape(1, d)
        g2 = norm2_g[l].reshape(1, d)
        w_in_l = w_in[l].astype(BF16)
        qg = jnp.tile(q_norm_g[l], attn_w // HEAD_DIM).reshape(1, attn_w)
        kg = jnp.tile(k_norm_g[l], kv_w // HEAD_DIM).reshape(1, kv_w)
        w_grp = w_pool_grp[l].astype(BF16)
        p_scale = pool_scale[l].reshape(1, -1)
        sink = win_sink[l]
        wg = w_gate[l].astype(BF16)
        bg = b_gate[l]
        wb = w_branch[l].astype(BF16)
        wo = w_out[l].astype(BF16)
        wfg = w_ff_gate[l].astype(BF16)
        wfv = w_ff_val[l].astype(BF16)
        wfd = w_ff_down[l].astype(BF16)
        cw = conv_w[l]
        cb = conv_b[l].reshape(1, -1)

        h_l, u_l, wq_l, wk_l, wv_l, gq_l, gk_l, gv_l = _inproj(
            x, mod_l, g1, w_in_l, cos, sin, qg, kg, bd, rope=True, tl=tl)
        h_c, u_c, wq_c, wk_c, wv_c, gq_c, gk_c, gv_c = _inproj(
            ctx, mod_c, g1, w_in_l, cos_c, sin_c, qg, kg, bd, rope=False, tl=tl_c)

        y_win = _window_attention(sink, wq_l, wk_l, wv_l, wk_c, wv_c, qb=win_qb)
        bound = _score_bound(q_norm_g[l], k_norm_g[l])
        y_glob = _global_attention(bound, gq_l, gk_l, gv_l, gk_c, gv_c, tq=tq, kc=GLOBAL_KEY_CHUNK)
        x1, h2 = _merge(x, h_l, mod_l, g2, u_l, w_grp, p_scale, y_win, y_glob, wg, bg, wb, wo, tl=tl)

        if not last:
            yc_win, yc_glob = _context_attention(sink, wq_c, wk_c, wv_c, gq_c, gk_c, gv_c)
            c1, hc2 = _merge(ctx, h_c, mod_c, g2, u_c, w_grp, p_scale, yc_win, yc_glob, wg, bg, wb, wo, tl=tl_c)
            ctx = _ffn(hc2, c1, mod_c, wfg, wfv, cw, cb, wfd, fg, tl=tl_c, final=False)

        x = _ffn(h2, x1, mod_l, wfg, wfv, cw, cb, wfd, fg, tl=tl_ffn, final=last)
    return x
```
